```python
import jax, jax.numpy as jnp
from jax import lax
import numpy as np

D_MODEL = 4096
BATCH = 4
SEQ = 2048
DEPTH = 1

CHUNK = 64
Q_BLOCK = 128
D_PLE = 256
A_HEADS = 16
A_HEAD_DIM = 128
A_LATENT = 256
IDX_HEADS = 32
IDX_DIM = 64
TOPK_MAX = 256
B_HEADS = 16
B_HEAD_DIM = 128
PEER_HEADS = 8
PEER_KEYS = 128
PEER_QDIM = 256
PEER_TOPK = 16
N_EXPERTS = PEER_KEYS * PEER_KEYS
PEER_TOKEN_BLOCK = 128
N_BRANCH = 2
LN_EPS = 1e-5
IN_WIDTHS = (
    A_HEADS * A_HEAD_DIM,
    A_LATENT,
    IDX_HEADS * IDX_DIM,
    IDX_DIM,
    IDX_HEADS,
    B_HEADS * B_HEAD_DIM,
    B_HEADS * B_HEAD_DIM,
    B_HEADS * B_HEAD_DIM,
    B_HEADS,
)
IN_WIDTH = sum(IN_WIDTHS)

kernel_name = "hybrid_dsa_fox_peer_block"


def layer_norm(x, g, b):
    xf = x.astype(jnp.float32)
    mu = jnp.mean(xf, axis=-1, keepdims=True)
    var = jnp.mean(jnp.square(xf - mu), axis=-1, keepdims=True)
    y = (xf - mu) * lax.rsqrt(var + LN_EPS) * g.astype(jnp.float32) + b.astype(jnp.float32)
    return y.astype(x.dtype)


def rms_norm(x, g):
    xf = x.astype(jnp.float32)
    y = xf * lax.rsqrt(jnp.mean(jnp.square(xf), axis=-1, keepdims=True) + LN_EPS) * g.astype(jnp.float32)
    return y.astype(x.dtype)


def alibi_slopes(n):
    return 2.0 ** (-8.0 * jnp.arange(1, n + 1, dtype=jnp.float32) / n)


def to_blocks(a, n_blocks):
    return a.reshape(a.shape[0], n_blocks, Q_BLOCK, *a.shape[2:]).swapaxes(0, 1)


def dsa_attention(q, c_kv, q_idx, k_idx, w_idx, w_uk, w_uv):
    bsz, seq = q.shape[0], q.shape[1]
    n_blocks = seq // Q_BLOCK
    k_sel = min(TOPK_MAX, seq // 4)
    pos = jnp.arange(seq, dtype=jnp.int32)
    slopes = alibi_slopes(A_HEADS)
    scale = A_HEAD_DIM ** -0.5
    idx_scale = (IDX_DIM ** -0.5) * (IDX_HEADS ** -0.5)

    def block(args):
        qb, qib, wib, tb = args
        rel = jax.nn.relu(jnp.einsum('bqhd,bsd->bqsh', qib, k_idx).astype(jnp.float32))
        score = jnp.einsum('bqsh,bqh->bqs', rel, wib.astype(jnp.float32)) * idx_scale
        chunk_end = (tb // CHUNK + 1) * CHUNK
        admissible = pos[None, :] < chunk_end[:, None]
        score = jnp.where(admissible[None], score, -jnp.inf)
        _, idx = lax.top_k(score, k_sel)
        valid = idx < chunk_end[None, :, None]
        c_sel = jax.vmap(lambda c, i: c[i])(c_kv, idx)
        q_lat = jnp.einsum('bqhd,chd->bqhc', qb, w_uk)
        logits = jnp.einsum('bqhc,bqkc->bqhk', q_lat, c_sel).astype(jnp.float32) * scale
        dist = jnp.abs(tb[None, :, None] - idx).astype(jnp.float32)
        logits = logits - slopes[None, None, :, None] * dist[:, :, None, :]
        logits = jnp.where(valid[:, :, None, :], logits, -jnp.inf)
        probs = jax.nn.softmax(logits, axis=-1).astype(c_sel.dtype)
        o_lat = jnp.einsum('bqhk,bqkc->bqhc', probs, c_sel)
        o = jnp.einsum('bqhc,chd->bqhd', o_lat, w_uv)
        return o.reshape(bsz, Q_BLOCK, A_HEADS * A_HEAD_DIM)

    out = lax.map(block, (to_blocks(q, n_blocks), to_blocks(q_idx, n_blocks),
                          to_blocks(w_idx, n_blocks), pos.reshape(n_blocks, Q_BLOCK)))
    return out.swapaxes(0, 1).reshape(bsz, seq, A_HEADS * A_HEAD_DIM)


def fox_attention(q, k, v, f_logit):
    bsz, seq = q.shape[0], q.shape[1]
    n_blocks = seq // Q_BLOCK
    pos = jnp.arange(seq, dtype=jnp.int32)
    scale = B_HEAD_DIM ** -0.5
    cum = jnp.cumsum(jax.nn.log_sigmoid(f_logit.astype(jnp.float32)), axis=1)
    cum_k = cum.transpose(0, 2, 1)

    def block(args):
        qb, cqb, tb = args
        logits = jnp.einsum('bqhd,bshd->bhqs', qb, k).astype(jnp.float32) * scale
        logits = logits + cqb.transpose(0, 2, 1)[..., None] - cum_k[:, :, None, :]
        causal = pos[None, :] <= tb[:, None]
        logits = jnp.where(causal[None, None], logits, -jnp.inf)
        probs = jax.nn.softmax(logits, axis=-1).astype(v.dtype)
        o = jnp.einsum('bhqs,bshd->bqhd', probs, v)
        return o.reshape(bsz, Q_BLOCK, B_HEADS * B_HEAD_DIM)

    out = lax.map(block, (to_blocks(q, n_blocks), to_blocks(cum, n_blocks),
                          pos.reshape(n_blocks, Q_BLOCK)))
    return out.swapaxes(0, 1).reshape(bsz, seq, B_HEADS * B_HEAD_DIM)


def peer_ffn(x, w_q, sub_keys, u_tab, v_tab):
    bsz, seq, d = x.shape
    half = PEER_QDIM // 2
    q = jnp.einsum('bsd,dhk->bshk', x, w_q)
    q1, q2 = q[..., :half], q[..., half:]
    s1 = jnp.einsum('bshk,hnk->bshn', q1, sub_keys[:, 0]).astype(jnp.float32)
    s2 = jnp.einsum('bshk,hnk->bshn', q2, sub_keys[:, 1]).astype(jnp.float32)
    v1, i1 = lax.top_k(s1, PEER_TOPK)
    v2, i2 = lax.top_k(s2, PEER_TOPK)
    cand = (v1[..., :, None] + v2[..., None, :]).reshape(bsz, seq, PEER_HEADS, PEER_TOPK * PEER_TOPK)
    vals, ci = lax.top_k(cand, PEER_TOPK)
    experts = (jnp.take_along_axis(i1, ci // PEER_TOPK, axis=-1) * PEER_KEYS
               + jnp.take_along_axis(i2, ci % PEER_TOPK, axis=-1))
    gates = jax.nn.softmax(vals, axis=-1)
    n_tok = bsz * seq
    m = PEER_HEADS * PEER_TOPK
    n_tb = n_tok // PEER_TOKEN_BLOCK
    xs = x.reshape(n_tb, PEER_TOKEN_BLOCK, d)
    es = experts.reshape(n_tb, PEER_TOKEN_BLOCK, m)
    gs = gates.reshape(n_tb, PEER_TOKEN_BLOCK, m).astype(x.dtype)

    def block(args):
        xb, eb, gb = args
        act = jax.nn.gelu(jnp.einsum('nd,nmd->nm', xb, u_tab[eb]), approximate=False)
        return jnp.einsum('nm,nmd->nd', gb * act, v_tab[eb])

    y = lax.map(block, (xs, es, gs))
    return y.reshape(bsz, seq, d)


def setup_inputs(seed: int = 0) -> dict:
    key = jax.random.key(seed)
    ks = jax.random.split(key, 24)
    f32 = jnp.float32
    beta = (8.0 * DEPTH) ** -0.25
    nrm = lambda k, shape, s: jax.random.normal(k, shape, f32) * s
    L = DEPTH
    return {
        "x": nrm(ks[0], (BATCH, SEQ, D_MODEL), 1.0),
        "p": nrm(ks[1], (DEPTH, BATCH, SEQ, D_PLE), 1.0),
        "w_in": nrm(ks[2], (L, D_MODEL, IN_WIDTH), D_MODEL ** -0.5),
        "b_forget": 1.0 + nrm(ks[3], (L, B_HEADS), 0.1),
        "g_latent": 1.0 + nrm(ks[4], (L, A_LATENT), 0.02),
        "w_uk": nrm(ks[5], (L, A_LATENT, A_HEADS, A_HEAD_DIM), A_LATENT ** -0.5),
        "w_uv": nrm(ks[6], (L, A_LATENT, A_HEADS, A_HEAD_DIM), A_LATENT ** -0.5),
        "w_branch_a": nrm(ks[7], (L, A_HEADS * A_HEAD_DIM, D_MODEL), (A_HEADS * A_HEAD_DIM) ** -0.5),
        "w_branch_b": nrm(ks[8], (L, B_HEADS * B_HEAD_DIM, D_MODEL), (B_HEADS * B_HEAD_DIM) ** -0.5),
        "w_gate": nrm(ks[9], (L, D_MODEL, N_BRANCH * D_MODEL), D_MODEL ** -0.5),
        "b_gate": nrm(ks[10], (L, N_BRANCH * D_MODEL), 0.02),
        "w_out": nrm(ks[11], (L, D_MODEL, D_MODEL), beta * D_MODEL ** -0.5),
        "ln1_g": 1.0 + nrm(ks[12], (L, D_MODEL), 0.02),
        "ln1_b": nrm(ks[13], (L, D_MODEL), 0.02),
        "peer_wq": nrm(ks[14], (L, D_MODEL, PEER_HEADS, PEER_QDIM), D_MODEL ** -0.5),
        "peer_subkeys": nrm(ks[15], (L, PEER_HEADS, 2, PEER_KEYS, PEER_QDIM // 2), (PEER_QDIM // 2) ** -0.5),
        "peer_u": nrm(ks[16], (L, N_EXPERTS, D_MODEL), D_MODEL ** -0.5),
        "peer_v": nrm(ks[17], (L, N_EXPERTS, D_MODEL), beta * PEER_HEADS ** -0.5),
        "w_ple": nrm(ks[18], (L, D_PLE, D_MODEL), D_PLE ** -0.5),
        "w_ple_gate": nrm(ks[19], (L, D_MODEL, D_MODEL), D_MODEL ** -0.5),
        "b_ple_gate": nrm(ks[20], (L, D_MODEL), 0.02),
        "ln2_g": 1.0 + nrm(ks[21], (L, D_MODEL), 0.02),
        "ln2_b": nrm(ks[22], (L, D_MODEL), 0.02),
    }


def reference(x, p, w_in, b_forget, g_latent, w_uk, w_uv, w_branch_a, w_branch_b,
              w_gate, b_gate, w_out, ln1_g, ln1_b, peer_wq, peer_subkeys, peer_u,
              peer_v, w_ple, w_ple_gate, b_ple_gate, ln2_g, ln2_b):
    bsz, seq, d = x.shape
    alpha = (2.0 * DEPTH) ** 0.25
    split_at = [int(o) for o in np.cumsum(IN_WIDTHS)[:-1]]
    h = x
    for i in range(DEPTH):
        u = h
        proj = u @ w_in[i]
        (qa, c_kv, q_idx, k_idx, w_idx, qb, kb, vb, f_logit) = jnp.split(proj, split_at, axis=-1)
        qa = qa.reshape(bsz, seq, A_HEADS, A_HEAD_DIM)
        c_kv = rms_norm(c_kv, g_latent[i])
        q_idx = q_idx.reshape(bsz, seq, IDX_HEADS, IDX_DIM)
        o_a = dsa_attention(qa, c_kv, q_idx, k_idx, w_idx, w_uk[i], w_uv[i])
        o_b = fox_attention(qb.reshape(bsz, seq, B_HEADS, B_HEAD_DIM),
                            kb.reshape(bsz, seq, B_HEADS, B_HEAD_DIM),
                            vb.reshape(bsz, seq, B_HEADS, B_HEAD_DIM),
                            f_logit + b_forget[i])
        gates = jax.nn.sigmoid(u @ w_gate[i] + b_gate[i]).reshape(bsz, seq, N_BRANCH, d)
        merged = gates[:, :, 0] * (o_a @ w_branch_a[i]) + gates[:, :, 1] * (o_b @ w_branch_b[i])
        h = layer_norm(alpha * h + merged @ w_out[i], ln1_g[i], ln1_b[i])
        ffn = peer_ffn(h, peer_wq[i], peer_subkeys[i], peer_u[i], peer_v[i])
        ple = jax.nn.sigmoid(h @ w_ple_gate[i] + b_ple_gate[i]) * (p[i] @ w_ple[i])
        h = layer_norm(alpha * h + ffn + ple, ln2_g[i], ln2_b[i])
    return h
```

```python
import functools

import jax
import jax.numpy as jnp
from jax import lax
from jax.experimental import pallas as pl
from jax.experimental.pallas import tpu as pltpu

F32 = jnp.float32
BF16 = jnp.bfloat16
I32 = jnp.int32

LANES = 128
CHUNK = 64
A_HEADS = 16
A_HEAD_DIM = 128
A_LATENT = 256
IDX_HEADS = 32
IDX_DIM = 64
TOPK_MAX = 256
B_HEADS = 16
B_HEAD_DIM = 128
PEER_TOPK = 16
LN_EPS = 1e-5
INT_MIN = -2 ** 31
MIB = 1024 * 1024


def _params(n_grid, vmem_mib):
    return pltpu.CompilerParams(dimension_semantics=("arbitrary",) * n_grid,
                                vmem_limit_bytes=vmem_mib * MIB)


def _dot(a, b):
    return jnp.dot(a, b, preferred_element_type=F32)


def _dot_nt(a, b):
    return lax.dot_general(a, b, (((1,), (1,)), ((), ())), preferred_element_type=F32)


def _mm_kernel(a_ref, b_ref, o_ref, *, groups):
    acc = _dot(a_ref[...], b_ref[...])
    if groups is None:
        o_ref[...] = acc.astype(o_ref.dtype)
    else:
        for g in range(groups):
            o_ref[g] = acc[:, g * LANES:(g + 1) * LANES].astype(o_ref.dtype)


def _matmul(a, b, out_dtype, bm, bn, head_major=False, name="mm"):
    m, k = a.shape
    n = b.shape[1]
    bm, bn = min(bm, m), min(bn, n)
    assert m % bm == 0 and n % bn == 0
    if head_major:
        out_shape = jax.ShapeDtypeStruct((n // LANES, m, LANES), out_dtype)
        out_spec = pl.BlockSpec((bn // LANES, bm, LANES), lambda j, i: (j, i, 0))
        groups = bn // LANES
    else:
        out_shape = jax.ShapeDtypeStruct((m, n), out_dtype)
        out_spec = pl.BlockSpec((bm, bn), lambda j, i: (i, j))
        groups = None
    return pl.pallas_call(
        functools.partial(_mm_kernel, groups=groups),
        grid=(n // bn, m // bm),
        in_specs=[pl.BlockSpec((bm, k), lambda j, i: (i, 0)),
                  pl.BlockSpec((k, bn), lambda j, i: (0, j))],
        out_specs=out_spec, out_shape=out_shape,
        compiler_params=_params(2, 56), name=name)(a, b)


def _mm_res_kernel(a_ref, b_ref, x_ref, o_ref, *, alpha):
    o_ref[...] = alpha * x_ref[...] + _dot(a_ref[...], b_ref[...])


def _matmul_residual(a, b, x, alpha, bm, bn):
    m, k = a.shape
    n = b.shape[1]
    bm, bn = min(bm, m), min(bn, n)
    return pl.pallas_call(
        functools.partial(_mm_res_kernel, alpha=alpha),
        grid=(n // bn, m // bm),
        in_specs=[pl.BlockSpec((bm, k), lambda j, i: (i, 0)),
                  pl.BlockSpec((k, bn), lambda j, i: (0, j)),
                  pl.BlockSpec((bm, bn), lambda j, i: (i, j))],
        out_specs=pl.BlockSpec((bm, bn), lambda j, i: (i, j)),
        out_shape=jax.ShapeDtypeStruct((m, n), F32),
        compiler_params=_params(2, 56), name="outproj")(a, b, x)


def _merge_kernel(x_ref, oa_ref, ob_ref, wga_ref, wgb_ref, bga_ref, bgb_ref, wba_ref, wbb_ref, o_ref):
    x = x_ref[...]
    ga = jax.nn.sigmoid(_dot(x, wga_ref[...]) + bga_ref[...])
    gb = jax.nn.sigmoid(_dot(x, wgb_ref[...]) + bgb_ref[...])
    ya = _dot(oa_ref[...], wba_ref[...])
    yb = _dot(ob_ref[...], wbb_ref[...])
    o_ref[...] = (ga * ya + gb * yb).astype(o_ref.dtype)


def _merge(xb, oa, ob, wga, wgb, bga, bgb, wba, wbb, bm, bn):
    m, d = xb.shape
    ka, kb = oa.shape[1], ob.shape[1]
    n = wga.shape[1]
    bm, bn = min(bm, m), min(bn, n)
    row = lambda i, j: (i, 0)
    col = lambda i, j: (0, j)
    return pl.pallas_call(
        _merge_kernel,
        grid=(m // bm, n // bn),
        in_specs=[pl.BlockSpec((bm, d), row), pl.BlockSpec((bm, ka), row), pl.BlockSpec((bm, kb), row),
                  pl.BlockSpec((d, bn), col), pl.BlockSpec((d, bn), col),
                  pl.BlockSpec((1, bn), col), pl.BlockSpec((1, bn), col),
                  pl.BlockSpec((ka, bn), col), pl.BlockSpec((kb, bn), col)],
        out_specs=pl.BlockSpec((bm, bn), lambda i, j: (i, j)),
        out_shape=jax.ShapeDtypeStruct((m, n), BF16),
        compiler_params=_params(2, 56), name="merge")(xb, oa, ob, wga, wgb, bga, bgb, wba, wbb)


def _ple_kernel(h_ref, p_ref, wg_ref, bg_ref, wp_ref, o_ref):
    g = jax.nn.sigmoid(_dot(h_ref[...], wg_ref[...]) + bg_ref[...])
    o_ref[...] = (g * _dot(p_ref[...], wp_ref[...])).astype(o_ref.dtype)


def _ple(hb, pb, wg, bg, wp, bm, bn):
    m, d = hb.shape
    dp = pb.shape[1]
    n = wg.shape[1]
    bm, bn = min(bm, m), min(bn, n)
    return pl.pallas_call(
        _ple_kernel,
        grid=(n // bn, m // bm),
        in_specs=[pl.BlockSpec((bm, d), lambda j, i: (i, 0)), pl.BlockSpec((bm, dp), lambda j, i: (i, 0)),
                  pl.BlockSpec((d, bn), lambda j, i: (0, j)), pl.BlockSpec((1, bn), lambda j, i: (0, j)),
                  pl.BlockSpec((dp, bn), lambda j, i: (0, j))],
        out_specs=pl.BlockSpec((bm, bn), lambda j, i: (i, j)),
        out_shape=jax.ShapeDtypeStruct((m, n), BF16),
        compiler_params=_params(2, 56), name="ple")(hb, pb, wg, bg, wp)


def _layer_norm_rows(v, g, b):
    mu = jnp.mean(v, axis=-1, keepdims=True)
    d = v - mu
    var = jnp.mean(d * d, axis=-1, keepdims=True)
    return d * lax.rsqrt(var + LN_EPS) * g + b


def _ln1_kernel(hp_ref, g_ref, b_ref, h_ref, hb_ref, ht_ref):
    y = _layer_norm_rows(hp_ref[...], g_ref[...], b_ref[...])
    h_ref[...] = y
    hb_ref[...] = y.astype(BF16)
    ht_ref[...] = y.T.astype(BF16)


def _ln1(hp, g, b, bm):
    m, d = hp.shape
    bm = min(bm, m)
    return pl.pallas_call(
        _ln1_kernel,
        grid=(m // bm,),
        in_specs=[pl.BlockSpec((bm, d), lambda i: (i, 0)),
                  pl.BlockSpec((1, d), lambda i: (0, 0)), pl.BlockSpec((1, d), lambda i: (0, 0))],
        out_specs=[pl.BlockSpec((bm, d), lambda i: (i, 0)), pl.BlockSpec((bm, d), lambda i: (i, 0)),
                   pl.BlockSpec((d, bm), lambda i: (0, i))],
        out_shape=[jax.ShapeDtypeStruct((m, d), F32), jax.ShapeDtypeStruct((m, d), BF16),
                   jax.ShapeDtypeStruct((d, m), BF16)],
        compiler_params=_params(1, 48), name="ln1")(hp, g, b)


def _ln2_kernel(h_ref, yt_ref, ple_ref, g_ref, b_ref, o_ref, *, alpha):
    v = alpha * h_ref[...] + yt_ref[...].T + ple_ref[...].astype(F32)
    o_ref[...] = _layer_norm_rows(v, g_ref[...], b_ref[...])


def _ln2(h, yt, ple, g, b, alpha, bm):
    m, d = h.shape
    bm = min(bm, m)
    return pl.pallas_call(
        functools.partial(_ln2_kernel, alpha=alpha),
        grid=(m // bm,),
        in_specs=[pl.BlockSpec((bm, d), lambda i: (i, 0)), pl.BlockSpec((d, bm), lambda i: (0, i)),
                  pl.BlockSpec((bm, d), lambda i: (i, 0)),
                  pl.BlockSpec((1, d), lambda i: (0, 0)), pl.BlockSpec((1, d), lambda i: (0, 0))],
        out_specs=pl.BlockSpec((bm, d), lambda i: (i, 0)),
        out_shape=jax.ShapeDtypeStruct((m, d), F32),
        compiler_params=_params(1, 48), name="ln2")(h, yt, ple, g, b)


def _prep_kernel(sm_ref, gl_ref, bf_ref, ckv_ref, k2_ref, w_ref, cum_ref, cumt_ref, *, idx_scale, blk):
    s = sm_ref.shape[0]
    c = sm_ref[:, 0:A_LATENT]
    ms = jnp.mean(c * c, axis=-1, keepdims=True)
    ckv_ref[...] = (c * lax.rsqrt(ms + LN_EPS) * gl_ref[...]).astype(BF16)
    o = A_LATENT
    k2_ref[0] = sm_ref[:, o:o + LANES].astype(BF16)
    k2_ref[1] = sm_ref[:, o + LANES:o + 2 * LANES].astype(BF16)
    w_ref[...] = sm_ref[:, o + 2 * LANES:o + 3 * LANES] * idx_scale
    f = sm_ref[:, o + 3 * LANES:o + 4 * LANES] + bf_ref[...]
    ls = jnp.minimum(f, 0.0) - jnp.log1p(jnp.exp(-jnp.abs(f)))
    r = lax.broadcasted_iota(I32, (blk, blk), 0)
    cc = lax.broadcasted_iota(I32, (blk, blk), 1)
    tri = jnp.where(r >= cc, 1.0, 0.0).astype(BF16)
    carry = jnp.zeros((1, LANES), F32)
    for kb in range(s // blk):
        xk = ls[kb * blk:(kb + 1) * blk]
        hi = xk.astype(BF16)
        r1 = xk - hi.astype(F32)
        mid = r1.astype(BF16)
        lo = (r1 - mid.astype(F32)).astype(BF16)
        ck = _dot(tri, hi) + _dot(tri, mid) + _dot(tri, lo) + carry
        cum_ref[kb * blk:(kb + 1) * blk, :] = ck
        carry = ck[blk - 1:blk, :]
    cumt_ref[...] = cum_ref[...].T


def _prep(small, g_latent, b_forget_pad, bsz, seq, idx_scale):
    n, w = small.shape
    blk = min(256, seq)
    return pl.pallas_call(
        functools.partial(_prep_kernel, idx_scale=idx_scale, blk=blk),
        grid=(bsz,),
        in_specs=[pl.BlockSpec((seq, w), lambda b: (b, 0)),
                  pl.BlockSpec((1, A_LATENT), lambda b: (0, 0)), pl.BlockSpec((1, LANES), lambda b: (0, 0))],
        out_specs=[pl.BlockSpec((None, seq, A_LATENT), lambda b: (b, 0, 0)),
                   pl.BlockSpec((None, 2, seq, LANES), lambda b: (b, 0, 0, 0)),
                   pl.BlockSpec((seq, LANES), lambda b: (b, 0)),
                   pl.BlockSpec((seq, LANES), lambda b: (b, 0)),
                   pl.BlockSpec((None, LANES, seq), lambda b: (b, 0, 0))],
        out_shape=[jax.ShapeDtypeStruct((bsz, seq, A_LATENT), BF16),
                   jax.ShapeDtypeStruct((bsz, 2, seq, LANES), BF16),
                   jax.ShapeDtypeStruct((n, LANES), F32),
                   jax.ShapeDtypeStruct((n, LANES), F32),
                   jax.ShapeDtypeStruct((bsz, LANES, seq), F32)],
        compiler_params=_params(1, 48), name="prep")(small, g_latent, b_forget_pad)


def _fox_kernel(q_ref, k_ref, v_ref, cumc_ref, cumr_ref, o_ref, *, tq, scale):
    h = pl.program_id(1)
    i = pl.program_id(2)
    q = q_ref[...]
    lane = lax.broadcasted_iota(I32, (tq, LANES), 1)
    cq = jnp.sum(jnp.where(lane == h, cumc_ref[...], 0.0), axis=1, keepdims=True)
    row = i * tq + lax.broadcasted_iota(I32, (tq, tq), 0)
    col0 = lax.broadcasted_iota(I32, (tq, tq), 1)

    def body(j, carry):
        m, l, acc = carry
        off = pl.multiple_of(j * tq, tq)
        k = k_ref[pl.ds(off, tq), :]
        v = v_ref[pl.ds(off, tq), :]
        s = _dot_nt(q, k) * scale + cq - cumr_ref[j]
        s = jnp.where(col0 + j * tq <= row, s, -jnp.inf)
        m_new = jnp.maximum(m, jnp.max(s, axis=1, keepdims=True))
        p = jnp.exp(s - m_new)
        a = jnp.exp(m - m_new)
        l = a * l + jnp.sum(p, axis=1, keepdims=True)
        acc = a * acc + _dot(p.astype(BF16), v)
        return m_new, l, acc

    m0 = jnp.full((tq, 1), -jnp.inf, F32)
    l0 = jnp.zeros((tq, 1), F32)
    a0 = jnp.zeros((tq, B_HEAD_DIM), F32)
    _, l, acc = lax.fori_loop(0, i + 1, body, (m0, l0, a0))
    o_ref[...] = (acc / l).astype(o_ref.dtype)


def _fox(big, cum, cumr, bsz, seq, tq, g_q, g_k, g_v):
    n = big.shape[1]
    tq = min(tq, seq)
    nq = seq // tq
    return pl.pallas_call(
        functools.partial(_fox_kernel, tq=tq, scale=B_HEAD_DIM ** -0.5),
        grid=(bsz, B_HEADS, nq),
        in_specs=[pl.BlockSpec((None, tq, LANES), lambda b, h, i: (g_q + h, b * nq + i, 0)),
                  pl.BlockSpec((None, seq, LANES), lambda b, h, i: (g_k + h, b, 0)),
                  pl.BlockSpec((None, seq, LANES), lambda b, h, i: (g_v + h, b, 0)),
                  pl.BlockSpec((tq, LANES), lambda b, h, i: (b * nq + i, 0)),
                  pl.BlockSpec((None, None, nq, 1, tq), lambda b, h, i: (b, h, 0, 0, 0))],
        out_specs=pl.BlockSpec((tq, LANES), lambda b, h, i: (b * nq + i, h)),
        out_shape=jax.ShapeDtypeStruct((n, B_HEADS * B_HEAD_DIM), BF16),
        compiler_params=_params(3, 48), name="fox")(big, big, big, cum, cumr)


def _dsa_kernel(slopes_ref, qi_ref, qa_ref, k2_ref, w_ref, ckv_ref, wuk_ref, wuv_ref, o_ref,
                score_ref, key_ref, eq_ref, bias_ref, dist_ref, oh_ref, *, tq, seq, k_sel):
    i = pl.program_id(1)
    t0 = i * tq
    lane = lax.broadcasted_iota(I32, (tq, LANES), 1)

    score_ref[...] = jnp.zeros((tq, seq), F32)

    def idx_body(j, c):
        a = qi_ref[j]
        for half in range(2):
            r = _dot_nt(a, k2_ref[half])
            w = jnp.sum(jnp.where(lane == 2 * j + half, w_ref[...], 0.0), axis=1, keepdims=True)
            score_ref[...] += jnp.maximum(r, 0.0) * w
        return c

    lax.fori_loop(0, IDX_HEADS // 2, idx_body, 0)

    row = t0 + lax.broadcasted_iota(I32, (tq, seq), 0)
    col = lax.broadcasted_iota(I32, (tq, seq), 1)
    row1 = t0 + lax.broadcasted_iota(I32, (tq, 1), 0)
    chunk_end = (row1 // CHUNK + 1) * CHUNK
    adm = col < chunk_end

    bits = pltpu.bitcast(score_ref[...], I32)
    key = bits ^ ((bits >> 31) & 0x7FFFFFFF)
    key_ref[...] = jnp.where(adm, key, INT_MIN)

    def bis(it, tu):
        cand_u = tu | lax.shift_left(jnp.int32(1), 31 - it)
        cand_s = cand_u ^ INT_MIN
        cnt = jnp.sum(jnp.where(key_ref[...] >= cand_s, 1.0, 0.0), axis=1, keepdims=True)
        return jnp.where(cnt >= k_sel, cand_u, tu)

    tu = lax.fori_loop(0, 32, bis, jnp.zeros((tq, 1), I32))
    thr = tu ^ INT_MIN
    key = key_ref[...]
    gt = key > thr
    eq_ref[...] = jnp.where(key == thr, 1.0, 0.0)
    need = k_sel - jnp.sum(jnp.where(gt, 1.0, 0.0), axis=1, keepdims=True)

    nbits = seq.bit_length()

    def bis2(it, jj):
        cand = jj | lax.shift_left(jnp.int32(1), nbits - 1 - it)
        f = jnp.sum(jnp.where(col < cand, eq_ref[...], 0.0), axis=1, keepdims=True)
        return jnp.where((cand <= seq) & (f <= need), cand, jj)

    jstar = lax.fori_loop(0, nbits, bis2, jnp.zeros((tq, 1), I32))
    sel = jnp.where(col < jstar, eq_ref[...], 0.0) + jnp.where(gt, 1.0, 0.0)
    bias_ref[...] = jnp.where(adm & (sel > 0.0), 0.0, -jnp.inf)
    dist_ref[...] = jnp.abs(row - col).astype(F32)

    def head(h, c):
        qlat = _dot(qa_ref[h], wuk_ref[h]).astype(BF16)
        lg = _dot_nt(qlat, ckv_ref[...]) - slopes_ref[h] * dist_ref[...] + bias_ref[...]
        m = jnp.max(lg, axis=1, keepdims=True)
        p = jnp.exp(lg - m)
        l = jnp.sum(p, axis=1, keepdims=True)
        olat = (_dot(p.astype(BF16), ckv_ref[...]) / l).astype(BF16)
        oh_ref[h] = _dot(olat, wuv_ref[h]).astype(BF16)
        return c

    lax.fori_loop(0, A_HEADS, head, 0)
    for h in range(A_HEADS):
        o_ref[:, h * A_HEAD_DIM:(h + 1) * A_HEAD_DIM] = oh_ref[h]


def _dsa(slopes, big, k2, widx, ckv, wuk_t, wuv_h, bsz, seq, g_qi, g_qa):
    n = big.shape[1]
    tq = min(128, seq)
    nq = seq // tq
    k_sel = min(TOPK_MAX, seq // 4)
    npair = IDX_HEADS // 2
    return pl.pallas_call(
        functools.partial(_dsa_kernel, tq=tq, seq=seq, k_sel=k_sel),
        grid=(bsz, nq),
        in_specs=[pl.BlockSpec(memory_space=pltpu.SMEM),
                  pl.BlockSpec((npair, tq, LANES), lambda b, i: (g_qi // npair, b * nq + i, 0)),
                  pl.BlockSpec((A_HEADS, tq, LANES), lambda b, i: (g_qa // A_HEADS, b * nq + i, 0)),
                  pl.BlockSpec((None, 2, seq, LANES), lambda b, i: (b, 0, 0, 0)),
                  pl.BlockSpec((tq, LANES), lambda b, i: (b * nq + i, 0)),
                  pl.BlockSpec((None, seq, A_LATENT), lambda b, i: (b, 0, 0)),
                  pl.BlockSpec((A_HEADS, A_HEAD_DIM, A_LATENT), lambda b, i: (0, 0, 0)),
                  pl.BlockSpec((A_HEADS, A_LATENT, A_HEAD_DIM), lambda b, i: (0, 0, 0))],
        out_specs=pl.BlockSpec((tq, A_HEADS * A_HEAD_DIM), lambda b, i: (b * nq + i, 0)),
        out_shape=jax.ShapeDtypeStruct((n, A_HEADS * A_HEAD_DIM), BF16),
        scratch_shapes=[pltpu.VMEM((tq, seq), F32), pltpu.VMEM((tq, seq), I32), pltpu.VMEM((tq, seq), F32),
                        pltpu.VMEM((tq, seq), F32), pltpu.VMEM((tq, seq), F32),
                        pltpu.VMEM((A_HEADS, tq, A_HEAD_DIM), BF16)],
        compiler_params=_params(2, 48), name="dsa")(slopes, big, big, k2, widx, ckv, wuk_t, wuv_h)


def _topk_desc(x, k):
    out = []
    cur = x
    for _ in range(k):
        m = jnp.max(cur, axis=0, keepdims=True)
        out.append(m)
        cur = jnp.where(cur == m, -jnp.inf, cur)
    return out


def _route_kernel(qp_ref, sk_ref, s1_ref, e1_ref, s2_ref, e2_ref, thr_ref, v2_ref, cand_ref, *, heads):
    for h in range(heads):
        s1 = _dot_nt(sk_ref[h, 0], qp_ref[2 * h])
        s2 = _dot_nt(sk_ref[h, 1], qp_ref[2 * h + 1])
        v1 = _topk_desc(s1, PEER_TOPK)
        v2 = _topk_desc(s2, PEER_TOPK)
        for a in range(PEER_TOPK):
            v2_ref[a:a + 1, :] = v2[a]
        for a in range(PEER_TOPK):
            cand_ref[a * PEER_TOPK:(a + 1) * PEER_TOPK, :] = v1[a] + v2_ref[...]
        cand = cand_ref[...]
        t = _topk_desc(cand, PEER_TOPK)[-1]
        mx = v1[0] + v2[0]
        z = jnp.sum(jnp.where(cand >= t, jnp.exp(cand - mx), 0.0), axis=0, keepdims=True)
        s1_ref[h] = s1
        s2_ref[h] = s2
        e1_ref[h] = jnp.exp(s1 - v1[0]) / z
        e2_ref[h] = jnp.exp(s2 - v2[0])
        thr_ref[h:h + 1, :] = t


def _route(qp, sk, tn):
    g, n, _ = qp.shape
    heads = g // 2
    keys = sk.shape[2]
    tn = min(tn, n)
    hk = pl.BlockSpec((heads, keys, tn), lambda i: (0, 0, i))
    sd = jax.ShapeDtypeStruct((heads, keys, n), F32)
    return pl.pallas_call(
        functools.partial(_route_kernel, heads=heads),
        grid=(n // tn,),
        in_specs=[pl.BlockSpec((g, tn, LANES), lambda i: (0, i, 0)),
                  pl.BlockSpec(sk.shape, lambda i: (0, 0, 0, 0))],
        out_specs=[hk, hk, hk, hk, pl.BlockSpec((heads, tn), lambda i: (0, i))],
        out_shape=[sd, sd, sd, sd, jax.ShapeDtypeStruct((heads, n), F32)],
        scratch_shapes=[pltpu.VMEM((PEER_TOPK, tn), F32), pltpu.VMEM((PEER_TOPK * PEER_TOPK, tn), F32)],
        compiler_params=_params(1, 48), name="route")(qp, sk)


def _expert_kernel(ht_ref, u_ref, vt_ref, s1_ref, e1_ref, s2_ref, e2_ref, thr_ref, y_ref, a_ref, *, heads, rows):
    e = pl.program_id(1)
    act = _dot(u_ref[...], ht_ref[...])
    keys = s2_ref.shape[1]
    for r in range(rows):
        a = act[r * keys:(r + 1) * keys, :]
        w = jnp.zeros_like(a)
        for h in range(heads):
            s = s1_ref[h * rows + r:h * rows + r + 1, :] + s2_ref[h]
            g = e1_ref[h * rows + r:h * rows + r + 1, :] * e2_ref[h]
            w = w + jnp.where(s >= thr_ref[h:h + 1, :], g, 0.0)
        gelu = 0.5 * a * (1.0 + lax.erf(a * (2.0 ** -0.5)))
        a_ref[r * keys:(r + 1) * keys, :] = (w * gelu).astype(BF16)
    contrib = _dot(vt_ref[...], a_ref[...])

    @pl.when(e == 0)
    def _():
        y_ref[...] = contrib

    @pl.when(e > 0)
    def _():
        y_ref[...] += contrib


def _experts(ht, u, vt, s1r, e1r, s2, e2, thr, tn, e_blk):
    d, n = ht.shape
    ne = u.shape[0]
    heads, keys, _ = s2.shape
    rows = e_blk // keys
    tn = min(tn, n)
    return pl.pallas_call(
        functools.partial(_expert_kernel, heads=heads, rows=rows),
        grid=(n // tn, ne // e_blk),
        in_specs=[pl.BlockSpec((d, tn), lambda i, e: (0, i)),
                  pl.BlockSpec((e_blk, d), lambda i, e: (e, 0)),
                  pl.BlockSpec((d, e_blk), lambda i, e: (0, e)),
                  pl.BlockSpec((None, heads * rows, tn), lambda i, e: (e, 0, i)),
                  pl.BlockSpec((None, heads * rows, tn), lambda i, e: (e, 0, i)),
                  pl.BlockSpec((heads, keys, tn), lambda i, e: (0, 0, i)),
                  pl.BlockSpec((heads, keys, tn), lambda i, e: (0, 0, i)),
                  pl.BlockSpec((heads, tn), lambda i, e: (0, i))],
        out_specs=pl.BlockSpec((d, tn), lambda i, e: (0, i)),
        out_shape=jax.ShapeDtypeStruct((d, n), F32),
        scratch_shapes=[pltpu.VMEM((e_blk, tn), BF16)],
        compiler_params=_params(2, 60), name="experts")(ht, u, vt, s1r, e1r, s2, e2, thr)


def _layer(h, hb, p_i, w_in, b_forget, g_latent, w_uk, w_uv, w_branch_a, w_branch_b, w_gate, b_gate, w_out,
           ln1_g, ln1_b, peer_wq, peer_subkeys, peer_u, peer_v, w_ple, w_ple_gate, b_ple_gate, ln2_g, ln2_b,
           bsz, seq, alpha):
    n, d = h.shape
    wa, wl, wi = A_HEADS * A_HEAD_DIM, A_LATENT, IDX_HEADS * IDX_DIM
    wb = B_HEADS * B_HEAD_DIM
    o = [0, wa, wa + wl, wa + wl + wi, wa + wl + wi + IDX_DIM, wa + wl + wi + IDX_DIM + IDX_HEADS]
    o += [o[-1] + wb, o[-1] + 2 * wb, o[-1] + 3 * wb, o[-1] + 3 * wb + B_HEADS]
    seg = [w_in[:, o[k]:o[k + 1]] for k in range(9)]
    (w_qa, w_ckv, w_qi, w_ki, w_wi, w_qb, w_kb, w_vb, w_f) = seg
    zeros = lambda c: jnp.zeros((d, c), w_in.dtype)
    w_big = jnp.concatenate([w_qa, w_qi, w_qb, w_kb, w_vb], axis=1).astype(BF16)
    w_small = jnp.concatenate(
        [w_ckv, w_ki, zeros(LANES - IDX_DIM), zeros(LANES - IDX_DIM), w_ki,
         w_wi, zeros(LANES - IDX_HEADS), w_f, zeros(LANES - B_HEADS)], axis=1).astype(BF16)
    g_qa, g_qi = 0, wa // LANES
    g_qb = g_qi + wi // LANES
    g_kb, g_vb = g_qb + wb // LANES, g_qb + 2 * wb // LANES

    big = _matmul(hb, w_big, BF16, 1024, 1024, head_major=True, name="proj_big")
    small = _matmul(hb, w_small, F32, 1024, w_small.shape[1], name="proj_small")

    bf_pad = jnp.zeros((1, LANES), F32).at[0, :B_HEADS].set(b_forget)
    idx_scale = (IDX_DIM ** -0.5) * (IDX_HEADS ** -0.5)
    ckv, k2, widx, cum, cumt = _prep(small, g_latent.reshape(1, -1), bf_pad, bsz, seq, idx_scale)

    tq = min(512, seq)
    cumr = cumt[:, :B_HEADS, :].reshape(bsz, B_HEADS, seq // tq, 1, tq)
    o_b = _fox(big, cum, cumr, bsz, seq, tq, g_qb, g_kb, g_vb)

    slopes = 2.0 ** (-8.0 * jnp.arange(1, A_HEADS + 1, dtype=F32) / A_HEADS)
    wuk_t = (jnp.transpose(w_uk, (1, 2, 0)) * (A_HEAD_DIM ** -0.5)).astype(BF16)
    wuv_h = jnp.transpose(w_uv, (1, 0, 2)).astype(BF16)
    o_a = _dsa(slopes, big, k2, widx, ckv, wuk_t, wuv_h, bsz, seq, g_qi, g_qa)

    wg = w_gate.astype(BF16)
    merged = _merge(hb, o_a, o_b, wg[:, :d], wg[:, d:], b_gate[:d].reshape(1, d), b_gate[d:].reshape(1, d),
                    w_branch_a.astype(BF16), w_branch_b.astype(BF16), 1024, 256)
    hpre = _matmul_residual(merged, w_out.astype(BF16), h, alpha, 1024, 1024)
    h1, h1b, h1t = _ln1(hpre, ln1_g.reshape(1, d), ln1_b.reshape(1, d), 256)

    heads, qd = peer_wq.shape[1], peer_wq.shape[2]
    keys = peer_subkeys.shape[2]
    qp = _matmul(h1b, peer_wq.reshape(d, heads * qd).astype(BF16), BF16, 1024, 1024, head_major=True,
                 name="peer_q")
    tn = min(512, n)
    e_blk = 512
    rows = e_blk // keys
    s1, e1, s2, e2, thr = _route(qp, peer_subkeys.astype(BF16), tn)
    regroup = lambda a: a.reshape(heads, keys // rows, rows, n).transpose(1, 0, 2, 3).reshape(
        keys // rows, heads * rows, n)
    yt = _experts(h1t, peer_u.astype(BF16), peer_v.T.astype(BF16), regroup(s1), regroup(e1), s2, e2, thr,
                  tn, e_blk)

    ple = _ple(h1b, p_i.astype(BF16), w_ple_gate.astype(BF16), b_ple_gate.reshape(1, d), w_ple.astype(BF16),
               1024, 1024)
    return _ln2(h1, yt, ple, ln2_g.reshape(1, d), ln2_b.reshape(1, d), alpha, 256)


def kernel(x, p, w_in, b_forget, g_latent, w_uk, w_uv, w_branch_a, w_branch_b, w_gate, b_gate, w_out, ln1_g,
           ln1_b, peer_wq, peer_subkeys, peer_u, peer_v, w_ple, w_ple_gate, b_ple_gate, ln2_g, ln2_b):
    bsz, seq, d = x.shape
    depth = w_in.shape[0]
    alpha = (2.0 * depth) ** 0.25
    h = x.reshape(bsz * seq, d)
    for i in range(depth):
        h = _layer(h, h.astype(BF16), p[i].reshape(bsz * seq, -1), w_in[i], b_forget[i], g_latent[i], w_uk[i],
                   w_uv[i], w_branch_a[i], w_branch_b[i], w_gate[i], b_gate[i], w_out[i], ln1_g[i], ln1_b[i],
                   peer_wq[i], peer_subkeys[i], peer_u[i], peer_v[i], w_ple[i], w_ple_gate[i], b_ple_gate[i],
                   ln2_g[i], ln2_b[i], bsz, seq, alpha)
    return h.reshape(bsz, seq, d)
```

```python
import functools

import jax
import jax.numpy as jnp
from jax import lax
from jax.experimental import pallas as pl
from jax.experimental.pallas import tpu as pltpu

F32 = jnp.float32
BF16 = jnp.bfloat16
I32 = jnp.int32

LANES = 128
SUBLANES = 8
CHUNK = 64
A_HEADS = 16
A_HEAD_DIM = 128
A_LATENT = 256
IDX_HEADS = 32
IDX_DIM = 64
TOPK_MAX = 256
B_HEADS = 16
B_HEAD_DIM = 128
PEER_TOPK = 16
LN_EPS = 1e-5
INT_MIN = -2 ** 31
MIB = 1024 * 1024


def _params(n_grid, vmem_mib):
    return pltpu.CompilerParams(dimension_semantics=("arbitrary",) * n_grid,
                                vmem_limit_bytes=vmem_mib * MIB)


def _dot(a, b):
    return jnp.dot(a, b, preferred_element_type=F32)


def _dot_nt(a, b):
    return lax.dot_general(a, b, (((1,), (1,)), ((), ())), preferred_element_type=F32)


def _mm_kernel(a_ref, b_ref, o_ref, *, groups):
    acc = _dot(a_ref[...], b_ref[...])
    if groups is None:
        o_ref[...] = acc.astype(o_ref.dtype)
    else:
        for g in range(groups):
            o_ref[g] = acc[:, g * LANES:(g + 1) * LANES].astype(o_ref.dtype)


def _matmul(a, b, out_dtype, bm, bn, head_major=False, name="mm"):
    m, k = a.shape
    n = b.shape[1]
    bm, bn = min(bm, m), min(bn, n)
    assert m % bm == 0 and n % bn == 0
    if head_major:
        out_shape = jax.ShapeDtypeStruct((n // LANES, m, LANES), out_dtype)
        out_spec = pl.BlockSpec((bn // LANES, bm, LANES), lambda j, i: (j, i, 0))
        groups = bn // LANES
    else:
        out_shape = jax.ShapeDtypeStruct((m, n), out_dtype)
        out_spec = pl.BlockSpec((bm, bn), lambda j, i: (i, j))
        groups = None
    return pl.pallas_call(
        functools.partial(_mm_kernel, groups=groups),
        grid=(n // bn, m // bm),
        in_specs=[pl.BlockSpec((bm, k), lambda j, i: (i, 0)),
                  pl.BlockSpec((k, bn), lambda j, i: (0, j))],
        out_specs=out_spec, out_shape=out_shape,
        compiler_params=_params(2, 56), name=name)(a, b)


def _mm_res_kernel(a_ref, b_ref, x_ref, o_ref, *, alpha):
    o_ref[...] = alpha * x_ref[...] + _dot(a_ref[...], b_ref[...])


def _matmul_residual(a, b, x, alpha, bm, bn):
    m, k = a.shape
    n = b.shape[1]
    bm, bn = min(bm, m), min(bn, n)
    return pl.pallas_call(
        functools.partial(_mm_res_kernel, alpha=alpha),
        grid=(n // bn, m // bm),
        in_specs=[pl.BlockSpec((bm, k), lambda j, i: (i, 0)),
                  pl.BlockSpec((k, bn), lambda j, i: (0, j)),
                  pl.BlockSpec((bm, bn), lambda j, i: (i, j))],
        out_specs=pl.BlockSpec((bm, bn), lambda j, i: (i, j)),
        out_shape=jax.ShapeDtypeStruct((m, n), F32),
        compiler_params=_params(2, 56), name="outproj")(a, b, x)


def _merge_kernel(x_ref, oa_ref, ob_ref, wga_ref, wgb_ref, bga_ref, bgb_ref, wba_ref, wbb_ref, o_ref):
    x = x_ref[...]
    ga = jax.nn.sigmoid(_dot(x, wga_ref[...]) + bga_ref[...])
    gb = jax.nn.sigmoid(_dot(x, wgb_ref[...]) + bgb_ref[...])
    ya = _dot(oa_ref[...], wba_ref[...])
    yb = _dot(ob_ref[...], wbb_ref[...])
    o_ref[...] = (ga * ya + gb * yb).astype(o_ref.dtype)


def _merge(xb, oa, ob, wga, wgb, bga, bgb, wba, wbb, bm, bn):
    m, d = xb.shape
    ka, kb = oa.shape[1], ob.shape[1]
    n = wga.shape[1]
    bm, bn = min(bm, m), min(bn, n)
    row = lambda i, j: (i, 0)
    col = lambda i, j: (0, j)
    return pl.pallas_call(
        _merge_kernel,
        grid=(m // bm, n // bn),
        in_specs=[pl.BlockSpec((bm, d), row), pl.BlockSpec((bm, ka), row), pl.BlockSpec((bm, kb), row),
                  pl.BlockSpec((d, bn), col), pl.BlockSpec((d, bn), col),
                  pl.BlockSpec((1, bn), col), pl.BlockSpec((1, bn), col),
                  pl.BlockSpec((ka, bn), col), pl.BlockSpec((kb, bn), col)],
        out_specs=pl.BlockSpec((bm, bn), lambda i, j: (i, j)),
        out_shape=jax.ShapeDtypeStruct((m, n), BF16),
        compiler_params=_params(2, 56), name="merge")(xb, oa, ob, wga, wgb, bga, bgb, wba, wbb)


def _ple_kernel(h_ref, p_ref, wg_ref, bg_ref, wp_ref, o_ref):
    g = jax.nn.sigmoid(_dot(h_ref[...], wg_ref[...]) + bg_ref[...])
    o_ref[...] = (g * _dot(p_ref[...], wp_ref[...])).astype(o_ref.dtype)


def _ple(hb, pb, wg, bg, wp, bm, bn):
    m, d = hb.shape
    dp = pb.shape[1]
    n = wg.shape[1]
    bm, bn = min(bm, m), min(bn, n)
    return pl.pallas_call(
        _ple_kernel,
        grid=(n // bn, m // bm),
        in_specs=[pl.BlockSpec((bm, d), lambda j, i: (i, 0)), pl.BlockSpec((bm, dp), lambda j, i: (i, 0)),
                  pl.BlockSpec((d, bn), lambda j, i: (0, j)), pl.BlockSpec((1, bn), lambda j, i: (0, j)),
                  pl.BlockSpec((dp, bn), lambda j, i: (0, j))],
        out_specs=pl.BlockSpec((bm, bn), lambda j, i: (i, j)),
        out_shape=jax.ShapeDtypeStruct((m, n), BF16),
        compiler_params=_params(2, 56), name="ple")(hb, pb, wg, bg, wp)


def _layer_norm_rows(v, g, b):
    mu = jnp.mean(v, axis=-1, keepdims=True)
    d = v - mu
    var = jnp.mean(d * d, axis=-1, keepdims=True)
    return d * lax.rsqrt(var + LN_EPS) * g + b


def _ln1_kernel(hp_ref, g_ref, b_ref, h_ref, hb_ref, ht_ref):
    y = _layer_norm_rows(hp_ref[...], g_ref[...], b_ref[...])
    h_ref[...] = y
    hb_ref[...] = y.astype(BF16)
    ht_ref[...] = y.T.astype(BF16)


def _ln1(hp, g, b, bm):
    m, d = hp.shape
    bm = min(bm, m)
    return pl.pallas_call(
        _ln1_kernel,
        grid=(m // bm,),
        in_specs=[pl.BlockSpec((bm, d), lambda i: (i, 0)),
                  pl.BlockSpec((1, d), lambda i: (0, 0)), pl.BlockSpec((1, d), lambda i: (0, 0))],
        out_specs=[pl.BlockSpec((bm, d), lambda i: (i, 0)), pl.BlockSpec((bm, d), lambda i: (i, 0)),
                   pl.BlockSpec((d, bm), lambda i: (0, i))],
        out_shape=[jax.ShapeDtypeStruct((m, d), F32), jax.ShapeDtypeStruct((m, d), BF16),
                   jax.ShapeDtypeStruct((d, m), BF16)],
        compiler_params=_params(1, 48), name="ln1")(hp, g, b)


def _ln2_kernel(h_ref, yt_ref, ple_ref, g_ref, b_ref, o_ref, *, alpha):
    v = alpha * h_ref[...] + yt_ref[...].T + ple_ref[...].astype(F32)
    o_ref[...] = _layer_norm_rows(v, g_ref[...], b_ref[...])


def _ln2(h, yt, ple, g, b, alpha, bm):
    m, d = h.shape
    bm = min(bm, m)
    return pl.pallas_call(
        functools.partial(_ln2_kernel, alpha=alpha),
        grid=(m // bm,),
        in_specs=[pl.BlockSpec((bm, d), lambda i: (i, 0)), pl.BlockSpec((d, bm), lambda i: (0, i)),
                  pl.BlockSpec((bm, d), lambda i: (i, 0)),
                  pl.BlockSpec((1, d), lambda i: (0, 0)), pl.BlockSpec((1, d), lambda i: (0, 0))],
        out_specs=pl.BlockSpec((bm, d), lambda i: (i, 0)),
        out_shape=jax.ShapeDtypeStruct((m, d), F32),
        compiler_params=_params(1, 48), name="ln2")(h, yt, ple, g, b)


def _prep_kernel(sm_ref, gl_ref, bf_ref, ckv_ref, kk_ref, w_ref, cum_ref, cumt_ref, *, idx_scale, blk):
    s = sm_ref.shape[0]
    c = sm_ref[:, 0:A_LATENT]
    ms = jnp.mean(c * c, axis=-1, keepdims=True)
    ckv_ref[...] = (c * lax.rsqrt(ms + LN_EPS) * gl_ref[...]).astype(BF16)
    o = A_LATENT
    kk_ref[...] = sm_ref[:, o:o + LANES].astype(BF16)
    w_ref[...] = sm_ref[:, o + LANES:o + 2 * LANES] * idx_scale
    f = sm_ref[:, o + 2 * LANES:o + 3 * LANES] + bf_ref[...]
    ls = jnp.minimum(f, 0.0) - jnp.log1p(jnp.exp(-jnp.abs(f)))
    r = lax.broadcasted_iota(I32, (blk, blk), 0)
    cc = lax.broadcasted_iota(I32, (blk, blk), 1)
    tri = jnp.where(r >= cc, 1.0, 0.0).astype(BF16)
    carry = jnp.zeros((1, LANES), F32)
    for kb in range(s // blk):
        xk = ls[kb * blk:(kb + 1) * blk]
        hi = xk.astype(BF16)
        r1 = xk - hi.astype(F32)
        mid = r1.astype(BF16)
        lo = (r1 - mid.astype(F32)).astype(BF16)
        ck = _dot(tri, hi) + _dot(tri, mid) + _dot(tri, lo) + carry
        cum_ref[kb * blk:(kb + 1) * blk, :] = ck
        carry = ck[blk - 1:blk, :]
    cumt_ref[...] = cum_ref[...].T


def _prep(small, g_latent, b_forget_pad, bsz, seq, idx_scale):
    n, w = small.shape
    blk = min(256, seq)
    return pl.pallas_call(
        functools.partial(_prep_kernel, idx_scale=idx_scale, blk=blk),
        grid=(bsz,),
        in_specs=[pl.BlockSpec((seq, w), lambda b: (b, 0)),
                  pl.BlockSpec((1, A_LATENT), lambda b: (0, 0)), pl.BlockSpec((1, LANES), lambda b: (0, 0))],
        out_specs=[pl.BlockSpec((None, seq, A_LATENT), lambda b: (b, 0, 0)),
                   pl.BlockSpec((None, seq, LANES), lambda b: (b, 0, 0)),
                   pl.BlockSpec((seq, LANES), lambda b: (b, 0)),
                   pl.BlockSpec((seq, LANES), lambda b: (b, 0)),
                   pl.BlockSpec((None, LANES, seq), lambda b: (b, 0, 0))],
        out_shape=[jax.ShapeDtypeStruct((bsz, seq, A_LATENT), BF16),
                   jax.ShapeDtypeStruct((bsz, seq, LANES), BF16),
                   jax.ShapeDtypeStruct((n, LANES), F32),
                   jax.ShapeDtypeStruct((n, LANES), F32),
                   jax.ShapeDtypeStruct((bsz, LANES, seq), F32)],
        compiler_params=_params(1, 48), name="prep")(small, g_latent, b_forget_pad)


FOX_HEADS_PER_STEP = 2


def _fox_kernel(q_ref, k_ref, v_ref, cumc_ref, cumr_ref, o_ref, *, tq, scale):
    hp = pl.program_id(1)
    i = pl.program_id(2)
    nh = FOX_HEADS_PER_STEP
    lane = lax.broadcasted_iota(I32, (tq, LANES), 1)
    qs = [(q_ref[e].astype(F32) * scale).astype(BF16) for e in range(nh)]
    cqs = [jnp.sum(jnp.where(lane == hp * nh + e, cumc_ref[...], 0.0), axis=1, keepdims=True)
           for e in range(nh)]

    def block(j, carry, diagonal):
        off = pl.multiple_of(j * tq, tq)
        out = []
        for e in range(nh):
            m, l, acc = carry[e]
            k = k_ref[e, pl.ds(off, tq), :]
            v = v_ref[e, pl.ds(off, tq), :]
            s = _dot_nt(qs[e], k) + (cqs[e] - cumr_ref[e, j])
            if diagonal:
                row = lax.broadcasted_iota(I32, (tq, tq), 0)
                col = lax.broadcasted_iota(I32, (tq, tq), 1)
                s = jnp.where(col <= row, s, -jnp.inf)
            m_new = jnp.maximum(m, jnp.max(s, axis=1, keepdims=True))
            p = jnp.exp(s - m_new)
            a = jnp.exp(m - m_new)
            l = a * l + jnp.sum(p, axis=1, keepdims=True)
            acc = a * acc + _dot(p.astype(BF16), v)
            out.append((m_new, l, acc))
        return tuple(out)

    init = (jnp.full((tq, 1), -jnp.inf, F32), jnp.zeros((tq, 1), F32), jnp.zeros((tq, B_HEAD_DIM), F32))
    carry = lax.fori_loop(0, i, functools.partial(block, diagonal=False), (init,) * nh)
    final = block(i, carry, True)
    for e in range(nh):
        _, l, acc = final[e]
        o_ref[:, e * B_HEAD_DIM:(e + 1) * B_HEAD_DIM] = (acc / l).astype(o_ref.dtype)


def _fox(big, cum, cumr, bsz, seq, tq, g_q, g_k, g_v):
    n = big.shape[1]
    tq = min(tq, seq)
    nq = seq // tq
    nh = FOX_HEADS_PER_STEP
    assert g_q % nh == 0 and g_k % nh == 0 and g_v % nh == 0 and B_HEADS % nh == 0
    return pl.pallas_call(
        functools.partial(_fox_kernel, tq=tq, scale=B_HEAD_DIM ** -0.5),
        grid=(bsz, B_HEADS // nh, nq),
        in_specs=[pl.BlockSpec((nh, tq, LANES), lambda b, h, i: (g_q // nh + h, b * nq + i, 0)),
                  pl.BlockSpec((nh, seq, LANES), lambda b, h, i: (g_k // nh + h, b, 0)),
                  pl.BlockSpec((nh, seq, LANES), lambda b, h, i: (g_v // nh + h, b, 0)),
                  pl.BlockSpec((tq, LANES), lambda b, h, i: (b * nq + i, 0)),
                  pl.BlockSpec((None, nh, nq, 1, tq), lambda b, h, i: (b, h, 0, 0, 0))],
        out_specs=pl.BlockSpec((tq, nh * B_HEAD_DIM), lambda b, h, i: (b * nq + i, h)),
        out_shape=jax.ShapeDtypeStruct((n, B_HEADS * B_HEAD_DIM), BF16),
        compiler_params=_params(3, 48), name="fox")(big, big, big, cum, cumr)


def _dsa_kernel(qi_ref, qa_ref, kk_ref, w_ref, ckv_ref, wuk_ref, wuv_ref, o_ref,
                qd_ref, wt_ref, key_ref, bias_ref, jstar_ref, qlat_ref, s_ref, mp_ref, lp_ref, p_ref, acc_ref,
                *, tq, seq, kc, k_sel):
    i = pl.program_id(1)
    t0 = i * tq
    nk = (t0 + tq + kc - 1) // kc
    npair = IDX_HEADS // 2
    rows_all = A_HEADS * tq
    slopes = [2.0 ** (-8.0 * (h + 1) / A_HEADS) for h in range(A_HEADS)]

    lane = lax.broadcasted_iota(I32, (tq, LANES), 1)
    for j in range(npair):
        a = qi_ref[j].astype(F32)
        qd_ref[j, 0:tq, :] = jnp.where(lane < IDX_DIM, a, 0.0).astype(BF16)
        qd_ref[j, tq:2 * tq, :] = jnp.where(lane >= IDX_DIM, a, 0.0).astype(BF16)
    wt_ref[...] = w_ref[...].T

    qpos = t0 + lax.broadcasted_iota(I32, (1, tq), 1)
    chunk_end = (qpos // CHUNK + 1) * CHUNK

    def idx_chunk(c, carry):
        kkc = kk_ref[c]
        acc = jnp.zeros((kc, tq), F32)
        for j in range(npair):
            out = _dot_nt(kkc, qd_ref[j])
            acc = acc + jnp.maximum(out[:, :tq], 0.0) * wt_ref[2 * j:2 * j + 1, :]
            acc = acc + jnp.maximum(out[:, tq:], 0.0) * wt_ref[2 * j + 1:2 * j + 2, :]
        kidx = c * kc + lax.broadcasted_iota(I32, (kc, tq), 0)
        bits = pltpu.bitcast(acc, I32)
        key = bits ^ ((bits >> 31) & 0x7FFFFFFF)
        key_ref[c] = jnp.where(kidx < chunk_end, key, INT_MIN)
        return carry

    lax.fori_loop(0, nk, idx_chunk, 0)

    sub = lax.broadcasted_iota(I32, (SUBLANES, tq), 0)

    def count(pred):
        nacc = 4

        def body(c, accs):
            kch = key_ref[c]
            accs = list(accs)
            for g in range(kc // SUBLANES):
                kg = kch[g * SUBLANES:(g + 1) * SUBLANES, :]
                accs[g % nacc] = accs[g % nacc] + jnp.where(pred(kg, c * kc + g * SUBLANES + sub), 1.0, 0.0)
            return tuple(accs)
        accs = lax.fori_loop(0, nk, body, (jnp.zeros((SUBLANES, tq), F32),) * nacc)
        acc = (accs[0] + accs[1]) + (accs[2] + accs[3])
        return jnp.broadcast_to(jnp.sum(acc, axis=0, keepdims=True), (SUBLANES, tq))

    def bis(it, tu):
        cand_u = tu | lax.shift_left(jnp.int32(1), 31 - it)
        cand_s = cand_u ^ INT_MIN
        cnt = count(lambda k, _: k >= cand_s)
        return jnp.where(cnt >= k_sel, cand_u, tu)

    tu = lax.fori_loop(0, 32, bis, jnp.zeros((SUBLANES, tq), I32))
    thr = tu ^ INT_MIN
    cnt_ge = count(lambda k, _: k >= thr)
    tie = jnp.where((cnt_ge > k_sel) & (thr != INT_MIN), 1, 0)
    jstar_ref[...] = jnp.full((SUBLANES, tq), seq, I32)

    @pl.when(jnp.max(tie) > 0)
    def _():
        need = k_sel - count(lambda k, _: k > thr)
        nbits = seq.bit_length()

        def bis2(it, jj):
            cand = jj | lax.shift_left(jnp.int32(1), nbits - 1 - it)
            f = count(lambda k, kidx: (k == thr) & (kidx < cand))
            return jnp.where((cand <= seq) & (f <= need), cand, jj)

        jstar_ref[...] = lax.fori_loop(0, nbits, bis2, jnp.zeros((SUBLANES, tq), I32))

    thr1 = thr[0:1, :]
    jst1 = jstar_ref[0:1, :]

    def bias_chunk(c, carry):
        kch = key_ref[c]
        kidx = c * kc + lax.broadcasted_iota(I32, (kc, tq), 0)
        sel = ((kch > thr1) | ((kch == thr1) & (kidx < jst1))) & (kidx < chunk_end)
        bias_ref[c] = jnp.where(sel, 0.0, -jnp.inf).T
        return carry

    lax.fori_loop(0, nk, bias_chunk, 0)

    for h in range(A_HEADS):
        qlat_ref[h * tq:(h + 1) * tq, :] = _dot(qa_ref[h], wuk_ref[h]).astype(BF16)
    mp_ref[...] = jnp.full((rows_all, LANES), -jnp.inf, F32)
    rowq = t0 + lax.broadcasted_iota(I32, (tq, kc), 0)
    colk = lax.broadcasted_iota(I32, (tq, kc), 1)

    def pass1(c, carry):
        s = _dot_nt(qlat_ref[...], ckv_ref[c])
        b = bias_ref[c]
        dist = jnp.abs(rowq - (colk + c * kc)).astype(F32)
        for h in range(A_HEADS):
            r = slice(h * tq, (h + 1) * tq)
            sh = s[r] - slopes[h] * dist + b
            s_ref[c, r, :] = sh
            mp = mp_ref[r, :]
            for u in range(kc // LANES):
                mp = jnp.maximum(mp, sh[:, u * LANES:(u + 1) * LANES])
            mp_ref[r, :] = mp
        return carry

    lax.fori_loop(0, nk, pass1, 0)
    m = jnp.max(mp_ref[...], axis=1, keepdims=True)
    mp_ref[...] = jnp.broadcast_to(m, (rows_all, LANES))
    lp_ref[...] = jnp.zeros((rows_all, LANES), F32)
    acc_ref[...] = jnp.zeros((rows_all, A_LATENT), F32)

    def pass2(c, carry):
        for h in range(A_HEADS):
            r = slice(h * tq, (h + 1) * tq)
            mrep = mp_ref[r, :]
            lp = lp_ref[r, :]
            for u in range(kc // LANES):
                p = jnp.exp(s_ref[c, r, u * LANES:(u + 1) * LANES] - mrep)
                lp = lp + p
                p_ref[r, u * LANES:(u + 1) * LANES] = p.astype(BF16)
            lp_ref[r, :] = lp
        acc_ref[...] += _dot(p_ref[...], ckv_ref[c])
        return carry

    lax.fori_loop(0, nk, pass2, 0)
    l = jnp.sum(lp_ref[...], axis=1, keepdims=True)
    olat = (acc_ref[...] / l).astype(BF16)
    for h in range(A_HEADS):
        o_ref[:, h * A_HEAD_DIM:(h + 1) * A_HEAD_DIM] = _dot(
            olat[h * tq:(h + 1) * tq], wuv_ref[h]).astype(o_ref.dtype)


def _dsa(big, kk, widx, ckv, wuk_t, wuv_h, bsz, seq, g_qi, g_qa):
    n = big.shape[1]
    tq = LANES
    assert seq % tq == 0
    kc = min(256, seq)
    nq, nkc = seq // tq, seq // kc
    k_sel = min(TOPK_MAX, seq // 4)
    npair = IDX_HEADS // 2
    rows_all = A_HEADS * tq
    kk = kk.reshape(bsz, nkc, kc, LANES)
    ckv = ckv.reshape(bsz, nkc, kc, A_LATENT)
    return pl.pallas_call(
        functools.partial(_dsa_kernel, tq=tq, seq=seq, kc=kc, k_sel=k_sel),
        grid=(bsz, nq),
        in_specs=[pl.BlockSpec((npair, tq, LANES), lambda b, i: (g_qi // npair, b * nq + i, 0)),
                  pl.BlockSpec((A_HEADS, tq, LANES), lambda b, i: (g_qa // A_HEADS, b * nq + i, 0)),
                  pl.BlockSpec((None, nkc, kc, LANES), lambda b, i: (b, 0, 0, 0)),
                  pl.BlockSpec((tq, LANES), lambda b, i: (b * nq + i, 0)),
                  pl.BlockSpec((None, nkc, kc, A_LATENT), lambda b, i: (b, 0, 0, 0)),
                  pl.BlockSpec((A_HEADS, A_HEAD_DIM, A_LATENT), lambda b, i: (0, 0, 0)),
                  pl.BlockSpec((A_HEADS, A_LATENT, A_HEAD_DIM), lambda b, i: (0, 0, 0))],
        out_specs=pl.BlockSpec((tq, A_HEADS * A_HEAD_DIM), lambda b, i: (b * nq + i, 0)),
        out_shape=jax.ShapeDtypeStruct((n, A_HEADS * A_HEAD_DIM), BF16),
        scratch_shapes=[pltpu.VMEM((npair, 2 * tq, LANES), BF16),
                        pltpu.VMEM((LANES, tq), F32),
                        pltpu.VMEM((nkc, kc, tq), I32),
                        pltpu.VMEM((nkc, tq, kc), F32),
                        pltpu.VMEM((SUBLANES, tq), I32),
                        pltpu.VMEM((rows_all, A_LATENT), BF16),
                        pltpu.VMEM((nkc, rows_all, kc), F32),
                        pltpu.VMEM((rows_all, LANES), F32),
                        pltpu.VMEM((rows_all, LANES), F32),
                        pltpu.VMEM((rows_all, kc), BF16),
                        pltpu.VMEM((rows_all, A_LATENT), F32)],
        compiler_params=_params(2, 56), name="dsa")(big, big, kk, widx, ckv, wuk_t, wuv_h)


def _topk_desc(x, k):
    out = []
    cur = x
    for _ in range(k):
        m = jnp.max(cur, axis=0, keepdims=True)
        out.append(m)
        cur = jnp.where(cur == m, -jnp.inf, cur)
    return out


def _route_kernel(qp_ref, sk_ref, s1_ref, e1_ref, s2_ref, e2_ref, thr_ref, v2_ref, cand_ref, *, heads, rows):
    keys = s2_ref.shape[1]
    for h in range(heads):
        s1 = _dot_nt(sk_ref[h, 0], qp_ref[2 * h])
        s2 = _dot_nt(sk_ref[h, 1], qp_ref[2 * h + 1])
        v1 = _topk_desc(s1, PEER_TOPK)
        v2 = _topk_desc(s2, PEER_TOPK)
        for a in range(PEER_TOPK):
            v2_ref[a:a + 1, :] = v2[a]
        off = 0
        for a in range(PEER_TOPK):
            nb = PEER_TOPK // (a + 1)
            cand_ref[off:off + nb, :] = v1[a] + v2_ref[0:nb, :]
            off += nb
        cand_ref[off:, :] = jnp.full((cand_ref.shape[0] - off, cand_ref.shape[1]), -jnp.inf, F32)
        cand = cand_ref[...]
        t = _topk_desc(cand, PEER_TOPK)[-1]
        mx = v1[0] + v2[0]
        z = jnp.sum(jnp.where(cand >= t, jnp.exp(cand - mx), 0.0), axis=0, keepdims=True)
        e1 = jnp.exp(s1 - v1[0]) / z
        for blk in range(keys // rows):
            s1_ref[blk, h * rows:(h + 1) * rows, :] = s1[blk * rows:(blk + 1) * rows, :]
            e1_ref[blk, h * rows:(h + 1) * rows, :] = e1[blk * rows:(blk + 1) * rows, :]
        s2_ref[h] = s2
        e2_ref[h] = jnp.exp(s2 - v2[0])
        thr_ref[h:h + 1, :] = t


def _route(qp, sk, tn, rows):
    g, n, _ = qp.shape
    heads = g // 2
    keys = sk.shape[2]
    tn = min(tn, n)
    nblk = keys // rows
    ncand = sum(PEER_TOPK // (a + 1) for a in range(PEER_TOPK))
    ncand = -(-ncand // SUBLANES) * SUBLANES
    hk = pl.BlockSpec((heads, keys, tn), lambda i: (0, 0, i))
    rk = pl.BlockSpec((nblk, heads * rows, tn), lambda i: (0, 0, i))
    sd = jax.ShapeDtypeStruct((heads, keys, n), F32)
    rd = jax.ShapeDtypeStruct((nblk, heads * rows, n), F32)
    return pl.pallas_call(
        functools.partial(_route_kernel, heads=heads, rows=rows),
        grid=(n // tn,),
        in_specs=[pl.BlockSpec((g, tn, LANES), lambda i: (0, i, 0)),
                  pl.BlockSpec(sk.shape, lambda i: (0, 0, 0, 0))],
        out_specs=[rk, rk, hk, hk, pl.BlockSpec((heads, tn), lambda i: (0, i))],
        out_shape=[rd, rd, sd, sd, jax.ShapeDtypeStruct((heads, n), F32)],
        scratch_shapes=[pltpu.VMEM((PEER_TOPK, tn), F32), pltpu.VMEM((ncand, tn), F32)],
        compiler_params=_params(1, 48), name="route")(qp, sk)


def _expert_kernel(ht_ref, u_ref, vt_ref, s1_ref, e1_ref, s2_ref, e2_ref, thr_ref, y_ref, act_ref, a_ref,
                   *, heads, rows, ne, dchunk):
    t = pl.program_id(0)
    keys = s2_ref.shape[1]
    d, tn = y_ref.shape
    NEW, USE = 0, 1

    @pl.when(t == 0)
    def _():
        act_ref[...] = jnp.zeros(act_ref.shape, F32)
        a_ref[...] = jnp.zeros(a_ref.shape, BF16)
        y_ref[...] = jnp.zeros(y_ref.shape, F32)

    a_ref[USE] = a_ref[NEW]
    act_ref[USE] = act_ref[NEW]
    first = (t - 2) % ne == 0

    def stage3(c):
        rs = slice(c * dchunk, (c + 1) * dchunk)
        contrib = _dot(vt_ref[rs, :], a_ref[USE])
        y_ref[rs, :] = jnp.where(first, contrib, y_ref[rs, :] + contrib)

    def stage2(r):
        a = act_ref[USE, r * keys:(r + 1) * keys, :]
        w = jnp.zeros_like(a)
        for h in range(heads):
            s = s1_ref[h * rows + r:h * rows + r + 1, :] + s2_ref[h]
            g = e1_ref[h * rows + r:h * rows + r + 1, :] * e2_ref[h]
            w = w + jnp.where(s >= thr_ref[h:h + 1, :], g, 0.0)
        gelu = 0.5 * a * (1.0 + lax.erf(a * (2.0 ** -0.5)))
        a_ref[NEW, r * keys:(r + 1) * keys, :] = (w * gelu).astype(BF16)

    def stage1(half):
        cs = slice(half * (tn // 2), (half + 1) * (tn // 2))
        act_ref[NEW, :, cs] = _dot(u_ref[...], ht_ref[:, cs])

    nd = d // dchunk
    mxu = [(stage3, 0), (stage1, 0)] + [(stage3, c) for c in range(1, nd // 2 + 1)] + [(stage1, 1)]
    mxu += [(stage3, c) for c in range(nd // 2 + 1, nd)]
    vec = [(stage2, r) for r in range(rows)]
    while mxu or vec:
        for queue in (mxu, vec):
            if queue:
                fn, arg = queue.pop(0)
                fn(arg)


def _experts(ht, u, vt, s1r, e1r, s2, e2, thr, tn, e_blk):
    d, n = ht.shape
    heads, keys, _ = s2.shape
    rows = e_blk // keys
    tn = min(tn, n)
    ne = u.shape[0] // e_blk
    last = (n // tn) * ne - 1
    st1 = lambda t: jnp.minimum(t, last)
    st2 = lambda t: jnp.clip(t - 1, 0, last)
    st3 = lambda t: jnp.clip(t - 2, 0, last)
    return pl.pallas_call(
        functools.partial(_expert_kernel, heads=heads, rows=rows, ne=ne, dchunk=min(1024, d)),
        grid=(last + 3,),
        in_specs=[pl.BlockSpec((d, tn), lambda t: (0, st1(t) // ne)),
                  pl.BlockSpec((e_blk, d), lambda t: (st1(t) % ne, 0)),
                  pl.BlockSpec((d, e_blk), lambda t: (0, st3(t) % ne)),
                  pl.BlockSpec((None, heads * rows, tn), lambda t: (st2(t) % ne, 0, st2(t) // ne)),
                  pl.BlockSpec((None, heads * rows, tn), lambda t: (st2(t) % ne, 0, st2(t) // ne)),
                  pl.BlockSpec((heads, keys, tn), lambda t: (0, 0, st2(t) // ne)),
                  pl.BlockSpec((heads, keys, tn), lambda t: (0, 0, st2(t) // ne)),
                  pl.BlockSpec((heads, tn), lambda t: (0, st2(t) // ne))],
        out_specs=pl.BlockSpec((d, tn), lambda t: (0, st3(t) // ne)),
        out_shape=jax.ShapeDtypeStruct((d, n), F32),
        scratch_shapes=[pltpu.VMEM((2, e_blk, tn), F32), pltpu.VMEM((2, e_blk, tn), BF16)],
        compiler_params=_params(1, 60), name="experts")(ht, u, vt, s1r, e1r, s2, e2, thr)


def _layer(h, hb, p_i, w_in, b_forget, g_latent, w_uk, w_uv, w_branch_a, w_branch_b, w_gate, b_gate, w_out,
           ln1_g, ln1_b, peer_wq, peer_subkeys, peer_u, peer_v, w_ple, w_ple_gate, b_ple_gate, ln2_g, ln2_b,
           bsz, seq, alpha):
    n, d = h.shape
    wa, wl, wi = A_HEADS * A_HEAD_DIM, A_LATENT, IDX_HEADS * IDX_DIM
    wb = B_HEADS * B_HEAD_DIM
    o = [0, wa, wa + wl, wa + wl + wi, wa + wl + wi + IDX_DIM, wa + wl + wi + IDX_DIM + IDX_HEADS]
    o += [o[-1] + wb, o[-1] + 2 * wb, o[-1] + 3 * wb, o[-1] + 3 * wb + B_HEADS]
    seg = [w_in[:, o[k]:o[k + 1]] for k in range(9)]
    (w_qa, w_ckv, w_qi, w_ki, w_wi, w_qb, w_kb, w_vb, w_f) = seg
    zeros = lambda c: jnp.zeros((d, c), w_in.dtype)
    w_big = jnp.concatenate([w_qa, w_qi, w_qb, w_kb, w_vb], axis=1).astype(BF16)
    w_small = jnp.concatenate(
        [w_ckv, w_ki, w_ki, w_wi, zeros(LANES - IDX_HEADS), w_f, zeros(LANES - B_HEADS)], axis=1).astype(BF16)
    g_qa, g_qi = 0, wa // LANES
    g_qb = g_qi + wi // LANES
    g_kb, g_vb = g_qb + wb // LANES, g_qb + 2 * wb // LANES

    big = _matmul(hb, w_big, BF16, 1024, 1024, head_major=True, name="proj_big")
    small = _matmul(hb, w_small, F32, 1024, w_small.shape[1], name="proj_small")

    bf_pad = jnp.zeros((1, LANES), F32).at[0, :B_HEADS].set(b_forget)
    idx_scale = (IDX_DIM ** -0.5) * (IDX_HEADS ** -0.5)
    ckv, kk, widx, cum, cumt = _prep(small, g_latent.reshape(1, -1), bf_pad, bsz, seq, idx_scale)

    tq = min(512, seq)
    cumr = cumt[:, :B_HEADS, :].reshape(bsz, B_HEADS, seq // tq, 1, tq)
    o_b = _fox(big, cum, cumr, bsz, seq, tq, g_qb, g_kb, g_vb)

    wuk_t = (jnp.transpose(w_uk, (1, 2, 0)) * (A_HEAD_DIM ** -0.5)).astype(BF16)
    wuv_h = jnp.transpose(w_uv, (1, 0, 2)).astype(BF16)
    o_a = _dsa(big, kk, widx, ckv, wuk_t, wuv_h, bsz, seq, g_qi, g_qa)

    wg = w_gate.astype(BF16)
    merged = _merge(hb, o_a, o_b, wg[:, :d], wg[:, d:], b_gate[:d].reshape(1, d), b_gate[d:].reshape(1, d),
                    w_branch_a.astype(BF16), w_branch_b.astype(BF16), 1024, 256)
    hpre = _matmul_residual(merged, w_out.astype(BF16), h, alpha, 1024, 1024)
    h1, h1b, h1t = _ln1(hpre, ln1_g.reshape(1, d), ln1_b.reshape(1, d), 256)

    heads, qd = peer_wq.shape[1], peer_wq.shape[2]
    keys = peer_subkeys.shape[2]
    qp = _matmul(h1b, peer_wq.reshape(d, heads * qd).astype(BF16), BF16, 1024, 1024, head_major=True,
                 name="peer_q")
    tn = min(512, n)
    e_blk = 512
    s1r, e1r, s2, e2, thr = _route(qp, peer_subkeys.astype(BF16), LANES, e_blk // keys)
    yt = _experts(h1t, peer_u.astype(BF16), peer_v.T.astype(BF16), s1r, e1r, s2, e2, thr, tn, e_blk)

    ple = _ple(h1b, p_i.astype(BF16), w_ple_gate.astype(BF16), b_ple_gate.reshape(1, d), w_ple.astype(BF16),
               1024, 1024)
    return _ln2(h1, yt, ple, ln2_g.reshape(1, d), ln2_b.reshape(1, d), alpha, 256)


def kernel(x, p, w_in, b_forget, g_latent, w_uk, w_uv, w_branch_a, w_branch_b, w_gate, b_gate, w_out, ln1_g,
           ln1_b, peer_wq, peer_subkeys, peer_u, peer_v, w_ple, w_ple_gate, b_ple_gate, ln2_g, ln2_b):
    bsz, seq, d = x.shape
    depth = w_in.shape[0]
    alpha = (2.0 * depth) ** 0.25
    h = x.reshape(bsz * seq, d)
    for i in range(depth):
        h = _layer(h, h.astype(BF16), p[i].reshape(bsz * seq, -1), w_in[i], b_forget[i], g_latent[i], w_uk[i],
                   w_uv[i], w_branch_a[i], w_branch_b[i], w_gate[i], b_gate[i], w_out[i], ln1_g[i], ln1_b[i],
                   peer_wq[i], peer_subkeys[i], peer_u[i], peer_v[i], w_ple[i], w_ple_gate[i], b_ple_gate[i],
                   ln2_g[i], ln2_b[i], bsz, seq, alpha)
    return h.reshape(bsz, seq, d)
```

```python
import functools

import jax
import jax.numpy as jnp
from jax import lax
from jax.experimental import pallas as pl
from jax.experimental.pallas import tpu as pltpu

F32 = jnp.float32
BF16 = jnp.bfloat16
I32 = jnp.int32

LANES = 128
SUBLANES = 8
CHUNK = 64
A_HEADS = 16
A_HEAD_DIM = 128
A_LATENT = 256
IDX_HEADS = 32
IDX_DIM = 64
TOPK_MAX = 256
B_HEADS = 16
B_HEAD_DIM = 128
PEER_TOPK = 16
LN_EPS = 1e-5
INT_MIN = -2 ** 31
MIB = 1024 * 1024


def _params(n_grid, vmem_mib):
    return pltpu.CompilerParams(dimension_semantics=("arbitrary",) * n_grid,
                                vmem_limit_bytes=vmem_mib * MIB)


def _dot(a, b):
    return jnp.dot(a, b, preferred_element_type=F32)


def _dot_nt(a, b):
    return lax.dot_general(a, b, (((1,), (1,)), ((), ())), preferred_element_type=F32)


def _mm_kernel(a_ref, b_ref, o_ref, *, groups):
    acc = _dot(a_ref[...], b_ref[...])
    if groups is None:
        o_ref[...] = acc.astype(o_ref.dtype)
    else:
        for g in range(groups):
            o_ref[g] = acc[:, g * LANES:(g + 1) * LANES].astype(o_ref.dtype)


def _matmul(a, b, out_dtype, bm, bn, head_major=False, name="mm"):
    m, k = a.shape
    n = b.shape[1]
    bm, bn = min(bm, m), min(bn, n)
    assert m % bm == 0 and n % bn == 0
    if head_major:
        out_shape = jax.ShapeDtypeStruct((n // LANES, m, LANES), out_dtype)
        out_spec = pl.BlockSpec((bn // LANES, bm, LANES), lambda j, i: (j, i, 0))
        groups = bn // LANES
    else:
        out_shape = jax.ShapeDtypeStruct((m, n), out_dtype)
        out_spec = pl.BlockSpec((bm, bn), lambda j, i: (i, j))
        groups = None
    return pl.pallas_call(
        functools.partial(_mm_kernel, groups=groups),
        grid=(n // bn, m // bm),
        in_specs=[pl.BlockSpec((bm, k), lambda j, i: (i, 0)),
                  pl.BlockSpec((k, bn), lambda j, i: (0, j))],
        out_specs=out_spec, out_shape=out_shape,
        compiler_params=_params(2, 56), name=name)(a, b)


def _mm_res_kernel(a_ref, b_ref, x_ref, o_ref, *, alpha):
    o_ref[...] = alpha * x_ref[...] + _dot(a_ref[...], b_ref[...])


def _matmul_residual(a, b, x, alpha, bm, bn):
    m, k = a.shape
    n = b.shape[1]
    bm, bn = min(bm, m), min(bn, n)
    return pl.pallas_call(
        functools.partial(_mm_res_kernel, alpha=alpha),
        grid=(n // bn, m // bm),
        in_specs=[pl.BlockSpec((bm, k), lambda j, i: (i, 0)),
                  pl.BlockSpec((k, bn), lambda j, i: (0, j)),
                  pl.BlockSpec((bm, bn), lambda j, i: (i, j))],
        out_specs=pl.BlockSpec((bm, bn), lambda j, i: (i, j)),
        out_shape=jax.ShapeDtypeStruct((m, n), F32),
        compiler_params=_params(2, 56), name="outproj")(a, b, x)


def _merge_kernel(x_ref, oa_ref, ob_ref, wga_ref, wgb_ref, bga_ref, bgb_ref, wba_ref, wbb_ref, o_ref):
    x = x_ref[...]
    ga = jax.nn.sigmoid(_dot(x, wga_ref[...]) + bga_ref[...])
    gb = jax.nn.sigmoid(_dot(x, wgb_ref[...]) + bgb_ref[...])
    ya = _dot(oa_ref[...], wba_ref[...])
    yb = _dot(ob_ref[...], wbb_ref[...])
    o_ref[...] = (ga * ya + gb * yb).astype(o_ref.dtype)


def _merge(xb, oa, ob, wg, bg, wba, wbb, bm, bn):
    m, d = xb.shape
    ka, kb = oa.shape[1], ob.shape[1]
    n = wba.shape[1]
    bm, bn = min(bm, m), min(bn, n)
    nj = n // bn
    row = lambda i, j: (i, 0)
    col = lambda i, j: (0, j)
    colb = lambda i, j: (0, nj + j)
    return pl.pallas_call(
        _merge_kernel,
        grid=(m // bm, nj),
        in_specs=[pl.BlockSpec((bm, d), row), pl.BlockSpec((bm, ka), row), pl.BlockSpec((bm, kb), row),
                  pl.BlockSpec((d, bn), col), pl.BlockSpec((d, bn), colb),
                  pl.BlockSpec((1, bn), col), pl.BlockSpec((1, bn), colb),
                  pl.BlockSpec((ka, bn), col), pl.BlockSpec((kb, bn), col)],
        out_specs=pl.BlockSpec((bm, bn), lambda i, j: (i, j)),
        out_shape=jax.ShapeDtypeStruct((m, n), BF16),
        compiler_params=_params(2, 56), name="merge")(xb, oa, ob, wg, wg, bg, bg, wba, wbb)


def _ple_kernel(h_ref, p_ref, wg_ref, bg_ref, wp_ref, o_ref):
    g = jax.nn.sigmoid(_dot(h_ref[...], wg_ref[...]) + bg_ref[...])
    o_ref[...] = (g * _dot(p_ref[...], wp_ref[...])).astype(o_ref.dtype)


def _ple(hb, pb, wg, bg, wp, bm, bn):
    m, d = hb.shape
    dp = pb.shape[1]
    n = wg.shape[1]
    bm, bn = min(bm, m), min(bn, n)
    return pl.pallas_call(
        _ple_kernel,
        grid=(n // bn, m // bm),
        in_specs=[pl.BlockSpec((bm, d), lambda j, i: (i, 0)), pl.BlockSpec((bm, dp), lambda j, i: (i, 0)),
                  pl.BlockSpec((d, bn), lambda j, i: (0, j)), pl.BlockSpec((1, bn), lambda j, i: (0, j)),
                  pl.BlockSpec((dp, bn), lambda j, i: (0, j))],
        out_specs=pl.BlockSpec((bm, bn), lambda j, i: (i, j)),
        out_shape=jax.ShapeDtypeStruct((m, n), BF16),
        compiler_params=_params(2, 56), name="ple")(hb, pb, wg, bg, wp)


def _layer_norm_rows(v, g, b):
    mu = jnp.mean(v, axis=-1, keepdims=True)
    d = v - mu
    var = jnp.mean(d * d, axis=-1, keepdims=True)
    return d * lax.rsqrt(var + LN_EPS) * g + b


def _ln1_kernel(hp_ref, g_ref, b_ref, h_ref, hb_ref, ht_ref):
    y = _layer_norm_rows(hp_ref[...], g_ref[...], b_ref[...])
    h_ref[...] = y
    hb_ref[...] = y.astype(BF16)
    ht_ref[...] = y.T.astype(BF16)


def _ln1(hp, g, b, bm):
    m, d = hp.shape
    bm = min(bm, m)
    return pl.pallas_call(
        _ln1_kernel,
        grid=(m // bm,),
        in_specs=[pl.BlockSpec((bm, d), lambda i: (i, 0)),
                  pl.BlockSpec((1, d), lambda i: (0, 0)), pl.BlockSpec((1, d), lambda i: (0, 0))],
        out_specs=[pl.BlockSpec((bm, d), lambda i: (i, 0)), pl.BlockSpec((bm, d), lambda i: (i, 0)),
                   pl.BlockSpec((d, bm), lambda i: (0, i))],
        out_shape=[jax.ShapeDtypeStruct((m, d), F32), jax.ShapeDtypeStruct((m, d), BF16),
                   jax.ShapeDtypeStruct((d, m), BF16)],
        compiler_params=_params(1, 48), name="ln1")(hp, g, b)


def _ln2_kernel(h_ref, yt_ref, ple_ref, g_ref, b_ref, o_ref, *, alpha):
    v = alpha * h_ref[...] + yt_ref[...].T + ple_ref[...].astype(F32)
    o_ref[...] = _layer_norm_rows(v, g_ref[...], b_ref[...])


def _ln2(h, yt, ple, g, b, alpha, bm):
    m, d = h.shape
    bm = min(bm, m)
    return pl.pallas_call(
        functools.partial(_ln2_kernel, alpha=alpha),
        grid=(m // bm,),
        in_specs=[pl.BlockSpec((bm, d), lambda i: (i, 0)), pl.BlockSpec((d, bm), lambda i: (0, i)),
                  pl.BlockSpec((bm, d), lambda i: (i, 0)),
                  pl.BlockSpec((1, d), lambda i: (0, 0)), pl.BlockSpec((1, d), lambda i: (0, 0))],
        out_specs=pl.BlockSpec((bm, d), lambda i: (i, 0)),
        out_shape=jax.ShapeDtypeStruct((m, d), F32),
        compiler_params=_params(1, 48), name="ln2")(h, yt, ple, g, b)


def _prep_kernel(sm_ref, gl_ref, bf_ref, ckv_ref, kk_ref, w_ref, cum_ref, cumt_ref, *, idx_scale, blk):
    s = sm_ref.shape[0]
    c = sm_ref[:, 0:A_LATENT]
    ms = jnp.mean(c * c, axis=-1, keepdims=True)
    ckv_ref[...] = (c * lax.rsqrt(ms + LN_EPS) * gl_ref[...]).astype(BF16)
    o = A_LATENT
    kk_ref[...] = sm_ref[:, o:o + LANES].astype(BF16)
    w_ref[...] = sm_ref[:, o + LANES:o + 2 * LANES] * idx_scale
    f = sm_ref[:, o + 2 * LANES:o + 3 * LANES] + bf_ref[...]
    ls = jnp.minimum(f, 0.0) - jnp.log1p(jnp.exp(-jnp.abs(f)))
    r = lax.broadcasted_iota(I32, (blk, blk), 0)
    cc = lax.broadcasted_iota(I32, (blk, blk), 1)
    tri = jnp.where(r >= cc, 1.0, 0.0).astype(BF16)
    carry = jnp.zeros((1, LANES), F32)
    for kb in range(s // blk):
        xk = ls[kb * blk:(kb + 1) * blk]
        hi = xk.astype(BF16)
        r1 = xk - hi.astype(F32)
        mid = r1.astype(BF16)
        lo = (r1 - mid.astype(F32)).astype(BF16)
        ck = _dot(tri, hi) + _dot(tri, mid) + _dot(tri, lo) + carry
        cum_ref[kb * blk:(kb + 1) * blk, :] = ck
        carry = ck[blk - 1:blk, :]
    cumt_ref[...] = cum_ref[...].T


def _prep(small, g_latent, b_forget_pad, bsz, seq, idx_scale):
    n, w = small.shape
    blk = min(256, seq)
    return pl.pallas_call(
        functools.partial(_prep_kernel, idx_scale=idx_scale, blk=blk),
        grid=(bsz,),
        in_specs=[pl.BlockSpec((seq, w), lambda b: (b, 0)),
                  pl.BlockSpec((1, A_LATENT), lambda b: (0, 0)), pl.BlockSpec((1, LANES), lambda b: (0, 0))],
        out_specs=[pl.BlockSpec((None, seq, A_LATENT), lambda b: (b, 0, 0)),
                   pl.BlockSpec((None, seq, LANES), lambda b: (b, 0, 0)),
                   pl.BlockSpec((seq, LANES), lambda b: (b, 0)),
                   pl.BlockSpec((seq, LANES), lambda b: (b, 0)),
                   pl.BlockSpec((None, LANES, seq), lambda b: (b, 0, 0))],
        out_shape=[jax.ShapeDtypeStruct((bsz, seq, A_LATENT), BF16),
                   jax.ShapeDtypeStruct((bsz, seq, LANES), BF16),
                   jax.ShapeDtypeStruct((n, LANES), F32),
                   jax.ShapeDtypeStruct((n, LANES), F32),
                   jax.ShapeDtypeStruct((bsz, LANES, seq), F32)],
        compiler_params=_params(1, 48), name="prep")(small, g_latent, b_forget_pad)


FOX_HEADS_PER_STEP = 2


def _fox_kernel(q_ref, k_ref, v_ref, cumc_ref, cumr_ref, o_ref, *, tq, scale):
    hp = pl.program_id(1)
    i = pl.program_id(2)
    nh = FOX_HEADS_PER_STEP
    lane = lax.broadcasted_iota(I32, (tq, LANES), 1)
    qs = [(q_ref[e].astype(F32) * scale).astype(BF16) for e in range(nh)]
    cqs = [jnp.sum(jnp.where(lane == hp * nh + e, cumc_ref[...], 0.0), axis=1, keepdims=True)
           for e in range(nh)]

    def block(j, carry, diagonal):
        off = pl.multiple_of(j * tq, tq)
        out = []
        for e in range(nh):
            m, l, acc = carry[e]
            k = k_ref[e, pl.ds(off, tq), :]
            v = v_ref[e, pl.ds(off, tq), :]
            s = _dot_nt(qs[e], k) + (cqs[e] - cumr_ref[e, j])
            if diagonal:
                row = lax.broadcasted_iota(I32, (tq, tq), 0)
                col = lax.broadcasted_iota(I32, (tq, tq), 1)
                s = jnp.where(col <= row, s, -jnp.inf)
            m_new = jnp.maximum(m, jnp.max(s, axis=1, keepdims=True))
            p = jnp.exp(s - m_new)
            a = jnp.exp(m - m_new)
            l = a * l + jnp.sum(p, axis=1, keepdims=True)
            acc = a * acc + _dot(p.astype(BF16), v)
            out.append((m_new, l, acc))
        return tuple(out)

    init = (jnp.full((tq, 1), -jnp.inf, F32), jnp.zeros((tq, 1), F32), jnp.zeros((tq, B_HEAD_DIM), F32))
    carry = lax.fori_loop(0, i, functools.partial(block, diagonal=False), (init,) * nh)
    final = block(i, carry, True)
    for e in range(nh):
        _, l, acc = final[e]
        o_ref[:, e * B_HEAD_DIM:(e + 1) * B_HEAD_DIM] = (acc / l).astype(o_ref.dtype)


def _fox(big, cum, cumr, bsz, seq, tq, g_q, g_k, g_v):
    n = big.shape[1]
    tq = min(tq, seq)
    nq = seq // tq
    nh = FOX_HEADS_PER_STEP
    assert g_q % nh == 0 and g_k % nh == 0 and g_v % nh == 0 and B_HEADS % nh == 0
    return pl.pallas_call(
        functools.partial(_fox_kernel, tq=tq, scale=B_HEAD_DIM ** -0.5),
        grid=(bsz, B_HEADS // nh, nq),
        in_specs=[pl.BlockSpec((nh, tq, LANES), lambda b, h, i: (g_q // nh + h, b * nq + i, 0)),
                  pl.BlockSpec((nh, seq, LANES), lambda b, h, i: (g_k // nh + h, b, 0)),
                  pl.BlockSpec((nh, seq, LANES), lambda b, h, i: (g_v // nh + h, b, 0)),
                  pl.BlockSpec((tq, LANES), lambda b, h, i: (b * nq + i, 0)),
                  pl.BlockSpec((None, nh, nq, 1, tq), lambda b, h, i: (b, h, 0, 0, 0))],
        out_specs=pl.BlockSpec((tq, nh * B_HEAD_DIM), lambda b, h, i: (b * nq + i, h)),
        out_shape=jax.ShapeDtypeStruct((n, B_HEADS * B_HEAD_DIM), BF16),
        compiler_params=_params(3, 48), name="fox")(big, big, big, cum, cumr)


def _dsa_kernel(qi_ref, qa_ref, kk_ref, w_ref, ckv_ref, wuk_ref, wuv_ref, o_ref,
                qd_ref, wt_ref, key_ref, bias_ref, jstar_ref, qlat_ref, s_ref, mp_ref, lp_ref, p_ref, acc_ref,
                *, tq, seq, kc, k_sel):
    i = pl.program_id(1)
    t0 = i * tq
    nk = (t0 + tq + kc - 1) // kc
    npair = IDX_HEADS // 2
    rows_all = A_HEADS * tq
    slopes = [2.0 ** (-8.0 * (h + 1) / A_HEADS) for h in range(A_HEADS)]

    lane = lax.broadcasted_iota(I32, (tq, LANES), 1)
    for j in range(npair):
        a = qi_ref[j].astype(F32)
        qd_ref[j, 0:tq, :] = jnp.where(lane < IDX_DIM, a, 0.0).astype(BF16)
        qd_ref[j, tq:2 * tq, :] = jnp.where(lane >= IDX_DIM, a, 0.0).astype(BF16)
    wt_ref[...] = w_ref[...].T

    qpos = t0 + lax.broadcasted_iota(I32, (1, tq), 1)
    chunk_end = (qpos // CHUNK + 1) * CHUNK

    def idx_chunk(c, carry):
        kkc = kk_ref[c]
        acc = jnp.zeros((kc, tq), F32)
        for j in range(npair):
            out = _dot_nt(kkc, qd_ref[j])
            acc = acc + jnp.maximum(out[:, :tq], 0.0) * wt_ref[2 * j:2 * j + 1, :]
            acc = acc + jnp.maximum(out[:, tq:], 0.0) * wt_ref[2 * j + 1:2 * j + 2, :]
        kidx = c * kc + lax.broadcasted_iota(I32, (kc, tq), 0)
        bits = pltpu.bitcast(acc, I32)
        key = bits ^ ((bits >> 31) & 0x7FFFFFFF)
        key_ref[c] = jnp.where(kidx < chunk_end, key, INT_MIN)
        return carry

    lax.fori_loop(0, nk, idx_chunk, 0)

    sub = lax.broadcasted_iota(I32, (SUBLANES, tq), 0)

    def count(pred):
        nacc = 4

        def body(c, accs):
            kch = key_ref[c]
            accs = list(accs)
            for g in range(kc // SUBLANES):
                kg = kch[g * SUBLANES:(g + 1) * SUBLANES, :]
                accs[g % nacc] = accs[g % nacc] + jnp.where(pred(kg, c * kc + g * SUBLANES + sub), 1.0, 0.0)
            return tuple(accs)
        accs = lax.fori_loop(0, nk, body, (jnp.zeros((SUBLANES, tq), F32),) * nacc)
        acc = (accs[0] + accs[1]) + (accs[2] + accs[3])
        return jnp.broadcast_to(jnp.sum(acc, axis=0, keepdims=True), (SUBLANES, tq))

    def bis(it, tu):
        cand_u = tu | lax.shift_left(jnp.int32(1), 31 - it)
        cand_s = cand_u ^ INT_MIN
        cnt = count(lambda k, _: k >= cand_s)
        return jnp.where(cnt >= k_sel, cand_u, tu)

    tu = lax.fori_loop(0, 32, bis, jnp.zeros((SUBLANES, tq), I32))
    thr = tu ^ INT_MIN
    cnt_ge = count(lambda k, _: k >= thr)
    tie = jnp.where((cnt_ge > k_sel) & (thr != INT_MIN), 1, 0)
    jstar_ref[...] = jnp.full((SUBLANES, tq), seq, I32)

    @pl.when(jnp.max(tie) > 0)
    def _():
        need = k_sel - count(lambda k, _: k > thr)
        nbits = seq.bit_length()

        def bis2(it, jj):
            cand = jj | lax.shift_left(jnp.int32(1), nbits - 1 - it)
            f = count(lambda k, kidx: (k == thr) & (kidx < cand))
            return jnp.where((cand <= seq) & (f <= need), cand, jj)

        jstar_ref[...] = lax.fori_loop(0, nbits, bis2, jnp.zeros((SUBLANES, tq), I32))

    thr1 = thr[0:1, :]
    jst1 = jstar_ref[0:1, :]

    def bias_chunk(c, carry):
        kch = key_ref[c]
        kidx = c * kc + lax.broadcasted_iota(I32, (kc, tq), 0)
        sel = ((kch > thr1) | ((kch == thr1) & (kidx < jst1))) & (kidx < chunk_end)
        bias_ref[c] = jnp.where(sel, 0.0, -jnp.inf).T
        return carry

    lax.fori_loop(0, nk, bias_chunk, 0)

    for h in range(A_HEADS):
        qlat_ref[h * tq:(h + 1) * tq, :] = _dot(qa_ref[h], wuk_ref[h]).astype(BF16)
    mp_ref[...] = jnp.full((rows_all, LANES), -jnp.inf, F32)
    rowq = t0 + lax.broadcasted_iota(I32, (tq, kc), 0)
    colk = lax.broadcasted_iota(I32, (tq, kc), 1)

    def pass1(c, carry):
        s = _dot_nt(qlat_ref[...], ckv_ref[c])
        b = bias_ref[c]
        dist = jnp.abs(rowq - (colk + c * kc)).astype(F32)
        for h in range(A_HEADS):
            r = slice(h * tq, (h + 1) * tq)
            sh = s[r] - slopes[h] * dist + b
            s_ref[c, r, :] = sh
            mp = mp_ref[r, :]
            for u in range(kc // LANES):
                mp = jnp.maximum(mp, sh[:, u * LANES:(u + 1) * LANES])
            mp_ref[r, :] = mp
        return carry

    lax.fori_loop(0, nk, pass1, 0)
    m = jnp.max(mp_ref[...], axis=1, keepdims=True)
    mp_ref[...] = jnp.broadcast_to(m, (rows_all, LANES))
    lp_ref[...] = jnp.zeros((rows_all, LANES), F32)
    acc_ref[...] = jnp.zeros((rows_all, A_LATENT), F32)

    def pass2(c, carry):
        for h in range(A_HEADS):
            r = slice(h * tq, (h + 1) * tq)
            mrep = mp_ref[r, :]
            lp = lp_ref[r, :]
            for u in range(kc // LANES):
                p = jnp.exp(s_ref[c, r, u * LANES:(u + 1) * LANES] - mrep)
                lp = lp + p
                p_ref[r, u * LANES:(u + 1) * LANES] = p.astype(BF16)
            lp_ref[r, :] = lp
        acc_ref[...] += _dot(p_ref[...], ckv_ref[c])
        return carry

    lax.fori_loop(0, nk, pass2, 0)
    l = jnp.sum(lp_ref[...], axis=1, keepdims=True)
    olat = (acc_ref[...] / l).astype(BF16)
    for h in range(A_HEADS):
        o_ref[:, h * A_HEAD_DIM:(h + 1) * A_HEAD_DIM] = _dot(
            olat[h * tq:(h + 1) * tq], wuv_ref[h]).astype(o_ref.dtype)


def _dsa(big_i, big_a, kk, widx, ckv, wuk_t, wuv_h, bsz, seq):
    n = big_a.shape[1]
    tq = LANES
    assert seq % tq == 0
    kc = min(256, seq)
    nq, nkc = seq // tq, seq // kc
    k_sel = min(TOPK_MAX, seq // 4)
    npair = IDX_HEADS // 2
    rows_all = A_HEADS * tq
    kk = kk.reshape(bsz, nkc, kc, LANES)
    ckv = ckv.reshape(bsz, nkc, kc, A_LATENT)
    return pl.pallas_call(
        functools.partial(_dsa_kernel, tq=tq, seq=seq, kc=kc, k_sel=k_sel),
        grid=(bsz, nq),
        in_specs=[pl.BlockSpec((npair, tq, LANES), lambda b, i: (0, b * nq + i, 0)),
                  pl.BlockSpec((A_HEADS, tq, LANES), lambda b, i: (0, b * nq + i, 0)),
                  pl.BlockSpec((None, nkc, kc, LANES), lambda b, i: (b, 0, 0, 0)),
                  pl.BlockSpec((tq, LANES), lambda b, i: (b * nq + i, 0)),
                  pl.BlockSpec((None, nkc, kc, A_LATENT), lambda b, i: (b, 0, 0, 0)),
                  pl.BlockSpec((A_HEADS, A_HEAD_DIM, A_LATENT), lambda b, i: (0, 0, 0)),
                  pl.BlockSpec((A_HEADS, A_LATENT, A_HEAD_DIM), lambda b, i: (0, 0, 0))],
        out_specs=pl.BlockSpec((tq, A_HEADS * A_HEAD_DIM), lambda b, i: (b * nq + i, 0)),
        out_shape=jax.ShapeDtypeStruct((n, A_HEADS * A_HEAD_DIM), BF16),
        scratch_shapes=[pltpu.VMEM((npair, 2 * tq, LANES), BF16),
                        pltpu.VMEM((LANES, tq), F32),
                        pltpu.VMEM((nkc, kc, tq), I32),
                        pltpu.VMEM((nkc, tq, kc), F32),
                        pltpu.VMEM((SUBLANES, tq), I32),
                        pltpu.VMEM((rows_all, A_LATENT), BF16),
                        pltpu.VMEM((nkc, rows_all, kc), F32),
                        pltpu.VMEM((rows_all, LANES), F32),
                        pltpu.VMEM((rows_all, LANES), F32),
                        pltpu.VMEM((rows_all, kc), BF16),
                        pltpu.VMEM((rows_all, A_LATENT), F32)],
        compiler_params=_params(2, 56), name="dsa")(big_i, big_a, kk, widx, ckv, wuk_t, wuv_h)


def _topk_desc(x, k):
    out = []
    cur = x
    for _ in range(k):
        m = jnp.max(cur, axis=0, keepdims=True)
        out.append(m)
        cur = jnp.where(cur == m, -jnp.inf, cur)
    return out


def _route_kernel(qp_ref, sk_ref, s1_ref, e1_ref, s2_ref, e2_ref, thr_ref, v2_ref, cand_ref, *, heads, rows):
    keys = s2_ref.shape[1]
    for h in range(heads):
        s1 = _dot_nt(sk_ref[h, 0], qp_ref[2 * h])
        s2 = _dot_nt(sk_ref[h, 1], qp_ref[2 * h + 1])
        v1 = _topk_desc(s1, PEER_TOPK)
        v2 = _topk_desc(s2, PEER_TOPK)
        for a in range(PEER_TOPK):
            v2_ref[a:a + 1, :] = v2[a]
        off = 0
        for a in range(PEER_TOPK):
            nb = PEER_TOPK // (a + 1)
            cand_ref[off:off + nb, :] = v1[a] + v2_ref[0:nb, :]
            off += nb
        cand_ref[off:, :] = jnp.full((cand_ref.shape[0] - off, cand_ref.shape[1]), -jnp.inf, F32)
        cand = cand_ref[...]
        t = _topk_desc(cand, PEER_TOPK)[-1]
        mx = v1[0] + v2[0]
        z = jnp.sum(jnp.where(cand >= t, jnp.exp(cand - mx), 0.0), axis=0, keepdims=True)
        e1 = jnp.exp(s1 - v1[0]) / z
        for blk in range(keys // rows):
            s1_ref[blk, h * rows:(h + 1) * rows, :] = s1[blk * rows:(blk + 1) * rows, :]
            e1_ref[blk, h * rows:(h + 1) * rows, :] = e1[blk * rows:(blk + 1) * rows, :]
        s2_ref[h] = s2
        e2_ref[h] = jnp.exp(s2 - v2[0])
        thr_ref[h:h + 1, :] = t


def _route(qp, sk, tn, rows):
    g, n, _ = qp.shape
    heads = g // 2
    keys = sk.shape[2]
    tn = min(tn, n)
    nblk = keys // rows
    ncand = sum(PEER_TOPK // (a + 1) for a in range(PEER_TOPK))
    ncand = -(-ncand // SUBLANES) * SUBLANES
    hk = pl.BlockSpec((heads, keys, tn), lambda i: (0, 0, i))
    rk = pl.BlockSpec((nblk, heads * rows, tn), lambda i: (0, 0, i))
    sd = jax.ShapeDtypeStruct((heads, keys, n), F32)
    rd = jax.ShapeDtypeStruct((nblk, heads * rows, n), F32)
    return pl.pallas_call(
        functools.partial(_route_kernel, heads=heads, rows=rows),
        grid=(n // tn,),
        in_specs=[pl.BlockSpec((g, tn, LANES), lambda i: (0, i, 0)),
                  pl.BlockSpec(sk.shape, lambda i: (0, 0, 0, 0))],
        out_specs=[rk, rk, hk, hk, pl.BlockSpec((heads, tn), lambda i: (0, i))],
        out_shape=[rd, rd, sd, sd, jax.ShapeDtypeStruct((heads, n), F32)],
        scratch_shapes=[pltpu.VMEM((PEER_TOPK, tn), F32), pltpu.VMEM((ncand, tn), F32)],
        compiler_params=_params(1, 48), name="route")(qp, sk)


def _expert_kernel(ht_ref, u_ref, vt_ref, s1_ref, e1_ref, s2_ref, e2_ref, thr_ref, y_ref,
                   act_ref, a_ref, *, heads, rows, ne, dchunk, sb):
    t = pl.program_id(0)
    keys = s2_ref.shape[1]
    d, tn = y_ref.shape
    e_blk = act_ref.shape[1]
    act_new_ref, act_use_ref, a_new_ref, a_use_ref = act_ref.at[0], act_ref.at[1], a_ref.at[0], a_ref.at[1]

    @pl.when(t == 0)
    def _():
        act_new_ref[...] = jnp.zeros(act_new_ref.shape, F32)
        a_new_ref[...] = jnp.zeros(a_new_ref.shape, BF16)
        y_ref[...] = jnp.zeros(y_ref.shape, F32)

    a_use_ref[...] = a_new_ref[...]
    act_use_ref[...] = act_new_ref[...]
    first = (t - 2) % ne == 0

    def second_matmul(c, zero):
        rs = slice(c * dchunk, (c + 1) * dchunk)
        contrib = _dot(vt_ref[rs, :], a_use_ref[...])
        y_ref[rs, :] = jnp.where(first, contrib, y_ref[rs, :] + contrib)
        c0 = c * dchunk
        y_ref[c0:c0 + SUBLANES, 0:LANES] = y_ref[c0:c0 + SUBLANES, 0:LANES] + zero

    def gate(piece):
        lo = piece * sb
        r, q = lo // keys, lo % keys
        a = act_use_ref[lo:lo + sb, :]
        w = jnp.zeros_like(a)
        for h in range(heads):
            s = s1_ref[h * rows + r:h * rows + r + 1, :] + s2_ref[h, q:q + sb, :]
            g = e1_ref[h * rows + r:h * rows + r + 1, :] * e2_ref[h, q:q + sb, :]
            w = w + jnp.where(s >= thr_ref[h:h + 1, :], g, 0.0)
        gelu = 0.5 * a * (1.0 + lax.erf(a * (2.0 ** -0.5)))
        out = w * gelu
        a_new_ref[lo:lo + sb, :] = out.astype(BF16)
        tile = out[0:SUBLANES, 0:LANES]
        for i in range(sb // SUBLANES):
            for j in range(tn // LANES):
                if i or j:
                    tile = tile + out[i * SUBLANES:(i + 1) * SUBLANES, j * LANES:(j + 1) * LANES]
        return tile

    def first_matmul(half):
        c0 = half * (tn // 2)
        act_new_ref[:, c0:c0 + tn // 2] = _dot(u_ref[...], ht_ref[:, c0:c0 + tn // 2])

    def gates_then_zero(pieces):
        tile = gate(pieces[0])
        for piece in pieces[1:]:
            tile = tile + gate(piece)
        bits = pltpu.bitcast(tile, jnp.uint32)
        return pltpu.bitcast((bits >> 16) >> 16, F32)

    nd, npieces = d // dchunk, e_blk // sb
    per = npieces // nd
    for c in range(nd):
        if c == 0:
            first_matmul(0)
        if c == nd // 2:
            first_matmul(1)
        second_matmul(c, gates_then_zero(list(range(c * per, (c + 1) * per))))


def _experts(ht, u, vt, s1r, e1r, s2, e2, thr, tn, e_blk):
    d, n = ht.shape
    heads, keys, _ = s2.shape
    rows = e_blk // keys
    tn = min(tn, n)
    ne = u.shape[0] // e_blk
    last = (n // tn) * ne - 1
    st1 = lambda t: jnp.minimum(t, last)
    st2 = lambda t: jnp.clip(t - 1, 0, last)
    st3 = lambda t: jnp.clip(t - 2, 0, last)
    return pl.pallas_call(
        functools.partial(_expert_kernel, heads=heads, rows=rows, ne=ne, dchunk=min(256, d), sb=32),
        grid=(last + 3,),
        in_specs=[pl.BlockSpec((d, tn), lambda t: (0, st1(t) // ne)),
                  pl.BlockSpec((e_blk, d), lambda t: (st1(t) % ne, 0)),
                  pl.BlockSpec((d, e_blk), lambda t: (0, st3(t) % ne)),
                  pl.BlockSpec((None, heads * rows, tn), lambda t: (st2(t) % ne, 0, st2(t) // ne)),
                  pl.BlockSpec((None, heads * rows, tn), lambda t: (st2(t) % ne, 0, st2(t) // ne)),
                  pl.BlockSpec((heads, keys, tn), lambda t: (0, 0, st2(t) // ne)),
                  pl.BlockSpec((heads, keys, tn), lambda t: (0, 0, st2(t) // ne)),
                  pl.BlockSpec((heads, tn), lambda t: (0, st2(t) // ne))],
        out_specs=pl.BlockSpec((d, tn), lambda t: (0, st3(t) // ne)),
        out_shape=jax.ShapeDtypeStruct((d, n), F32),
        scratch_shapes=[pltpu.VMEM((2, e_blk, tn), F32), pltpu.VMEM((2, e_blk, tn), BF16)],
        compiler_params=_params(1, 60), name="experts")(ht, u, vt, s1r, e1r, s2, e2, thr)


def _layer(h, hb, p_i, w_in, b_forget, g_latent, w_uk, w_uv, w_branch_a, w_branch_b, w_gate, b_gate, w_out,
           ln1_g, ln1_b, peer_wq, peer_subkeys, peer_u, peer_v, w_ple, w_ple_gate, b_ple_gate, ln2_g, ln2_b,
           bsz, seq, alpha):
    n, d = h.shape
    wa, wl, wi = A_HEADS * A_HEAD_DIM, A_LATENT, IDX_HEADS * IDX_DIM
    wb = B_HEADS * B_HEAD_DIM
    o = [0, wa, wa + wl, wa + wl + wi, wa + wl + wi + IDX_DIM, wa + wl + wi + IDX_DIM + IDX_HEADS]
    o += [o[-1] + wb, o[-1] + 2 * wb, o[-1] + 3 * wb, o[-1] + 3 * wb + B_HEADS]
    seg = [w_in[:, o[k]:o[k + 1]] for k in range(9)]
    (w_qa, w_ckv, w_qi, w_ki, w_wi, w_qb, w_kb, w_vb, w_f) = seg
    zeros = lambda c: jnp.zeros((d, c), w_in.dtype)
    w_small = jnp.concatenate(
        [w_ckv, w_ki, w_ki, w_wi, zeros(LANES - IDX_HEADS), w_f, zeros(LANES - B_HEADS)], axis=1).astype(BF16)
    w_fox = w_in[:, o[5]:o[8]].astype(BF16)

    big_a = _matmul(hb, w_qa.astype(BF16), BF16, 1024, 1024, head_major=True, name="proj_qa")
    big_i = _matmul(hb, w_qi.astype(BF16), BF16, 1024, 1024, head_major=True, name="proj_qidx")
    big_b = _matmul(hb, w_fox, BF16, 1024, 1024, head_major=True, name="proj_fox")
    small = _matmul(hb, w_small, F32, 1024, w_small.shape[1], name="proj_small")

    bf_pad = jnp.zeros((1, LANES), F32).at[0, :B_HEADS].set(b_forget)
    idx_scale = (IDX_DIM ** -0.5) * (IDX_HEADS ** -0.5)
    ckv, kk, widx, cum, cumt = _prep(small, g_latent.reshape(1, -1), bf_pad, bsz, seq, idx_scale)

    tq = min(512, seq)
    cumr = cumt[:, :B_HEADS, :].reshape(bsz, B_HEADS, seq // tq, 1, tq)
    o_b = _fox(big_b, cum, cumr, bsz, seq, tq, 0, wb // LANES, 2 * wb // LANES)

    wuk_t = (jnp.transpose(w_uk, (1, 2, 0)) * (A_HEAD_DIM ** -0.5)).astype(BF16)
    wuv_h = jnp.transpose(w_uv, (1, 0, 2)).astype(BF16)
    o_a = _dsa(big_i, big_a, kk, widx, ckv, wuk_t, wuv_h, bsz, seq)

    merged = _merge(hb, o_a, o_b, w_gate.astype(BF16), b_gate.reshape(1, 2 * d),
                    w_branch_a.astype(BF16), w_branch_b.astype(BF16), 1024, 256)
    hpre = _matmul_residual(merged, w_out.astype(BF16), h, alpha, 1024, 1024)
    h1, h1b, h1t = _ln1(hpre, ln1_g.reshape(1, d), ln1_b.reshape(1, d), 256)

    heads, qd = peer_wq.shape[1], peer_wq.shape[2]
    keys = peer_subkeys.shape[2]
    qp = _matmul(h1b, peer_wq.reshape(d, heads * qd).astype(BF16), BF16, 1024, 1024, head_major=True,
                 name="peer_q")
    tn = min(512, n)
    e_blk = 512
    s1r, e1r, s2, e2, thr = _route(qp, peer_subkeys.astype(BF16), LANES, e_blk // keys)
    yt = _experts(h1t, peer_u.astype(BF16), peer_v.T.astype(BF16), s1r, e1r, s2, e2, thr, tn, e_blk)

    ple = _ple(h1b, p_i.astype(BF16), w_ple_gate.astype(BF16), b_ple_gate.reshape(1, d), w_ple.astype(BF16),
               1024, 1024)
    return _ln2(h1, yt, ple, ln2_g.reshape(1, d), ln2_b.reshape(1, d), alpha, 256)


def kernel(x, p, w_in, b_forget, g_latent, w_uk, w_uv, w_branch_a, w_branch_b, w_gate, b_gate, w_out, ln1_g,
           ln1_b, peer_wq, peer_subkeys, peer_u, peer_v, w_ple, w_ple_gate, b_ple_gate, ln2_g, ln2_b):
    bsz, seq, d = x.shape
    depth = w_in.shape[0]
    alpha = (2.0 * depth) ** 0.25
    h = x.reshape(bsz * seq, d)
    for i in range(depth):
        h = _layer(h, h.astype(BF16), p[i].reshape(bsz * seq, -1), w_in[i], b_forget[i], g_latent[i], w_uk[i],
                   w_uv[i], w_branch_a[i], w_branch_b[i], w_gate[i], b_gate[i], w_out[i], ln1_g[i], ln1_b[i],
                   peer_wq[i], peer_subkeys[i], peer_u[i], peer_v[i], w_ple[i], w_ple_gate[i], b_ple_gate[i],
                   ln2_g[i], ln2_b[i], bsz, seq, alpha)
    return h.reshape(bsz, seq, d)
```

```python
import functools

import jax
import jax.numpy as jnp
from jax import lax
from jax.experimental import pallas as pl
from jax.experimental.pallas import tpu as pltpu

F32 = jnp.float32
BF16 = jnp.bfloat16
I32 = jnp.int32

LANES = 128
SUBLANES = 8
CHUNK = 64
A_HEADS = 16
A_HEAD_DIM = 128
A_LATENT = 256
IDX_HEADS = 32
IDX_DIM = 64
TOPK_MAX = 256
B_HEADS = 16
B_HEAD_DIM = 128
PEER_TOPK = 16
LN_EPS = 1e-5
INT_MIN = -2 ** 31
MIB = 1024 * 1024


def _params(n_grid, vmem_mib):
    return pltpu.CompilerParams(dimension_semantics=("arbitrary",) * n_grid,
                                vmem_limit_bytes=vmem_mib * MIB)


def _dot(a, b):
    return jnp.dot(a, b, preferred_element_type=F32)


def _dot_nt(a, b):
    return lax.dot_general(a, b, (((1,), (1,)), ((), ())), preferred_element_type=F32)


def _mm_kernel(a_ref, b_ref, o_ref, *, groups):
    acc = _dot(a_ref[...], b_ref[...])
    if groups is None:
        o_ref[...] = acc.astype(o_ref.dtype)
    else:
        for g in range(groups):
            o_ref[g] = acc[:, g * LANES:(g + 1) * LANES].astype(o_ref.dtype)


def _matmul(a, b, out_dtype, bm, bn, head_major=False, name="mm"):
    m, k = a.shape
    n = b.shape[1]
    bm, bn = min(bm, m), min(bn, n)
    assert m % bm == 0 and n % bn == 0
    if head_major:
        out_shape = jax.ShapeDtypeStruct((n // LANES, m, LANES), out_dtype)
        out_spec = pl.BlockSpec((bn // LANES, bm, LANES), lambda j, i: (j, i, 0))
        groups = bn // LANES
    else:
        out_shape = jax.ShapeDtypeStruct((m, n), out_dtype)
        out_spec = pl.BlockSpec((bm, bn), lambda j, i: (i, j))
        groups = None
    return pl.pallas_call(
        functools.partial(_mm_kernel, groups=groups),
        grid=(n // bn, m // bm),
        in_specs=[pl.BlockSpec((bm, k), lambda j, i: (i, 0)),
                  pl.BlockSpec((k, bn), lambda j, i: (0, j))],
        out_specs=out_spec, out_shape=out_shape,
        compiler_params=_params(2, 56), name=name)(a, b)


def _mm_res_kernel(a_ref, b_ref, x_ref, o_ref, *, alpha):
    o_ref[...] = alpha * x_ref[...] + _dot(a_ref[...], b_ref[...])


def _matmul_residual(a, b, x, alpha, bm, bn):
    m, k = a.shape
    n = b.shape[1]
    bm, bn = min(bm, m), min(bn, n)
    return pl.pallas_call(
        functools.partial(_mm_res_kernel, alpha=alpha),
        grid=(n // bn, m // bm),
        in_specs=[pl.BlockSpec((bm, k), lambda j, i: (i, 0)),
                  pl.BlockSpec((k, bn), lambda j, i: (0, j)),
                  pl.BlockSpec((bm, bn), lambda j, i: (i, j))],
        out_specs=pl.BlockSpec((bm, bn), lambda j, i: (i, j)),
        out_shape=jax.ShapeDtypeStruct((m, n), F32),
        compiler_params=_params(2, 56), name="outproj")(a, b, x)


def _merge_kernel(x_ref, oa_ref, ob_ref, wga_ref, wgb_ref, bga_ref, bgb_ref, wba_ref, wbb_ref, o_ref):
    x = x_ref[...]
    ga = jax.nn.sigmoid(_dot(x, wga_ref[...]) + bga_ref[...])
    gb = jax.nn.sigmoid(_dot(x, wgb_ref[...]) + bgb_ref[...])
    ya = _dot(oa_ref[...], wba_ref[...])
    yb = _dot(ob_ref[...], wbb_ref[...])
    o_ref[...] = (ga * ya + gb * yb).astype(o_ref.dtype)


def _merge(xb, oa, ob, wg, bg, wba, wbb, bm, bn):
    m, d = xb.shape
    ka, kb = oa.shape[1], ob.shape[1]
    n = wba.shape[1]
    bm, bn = min(bm, m), min(bn, n)
    nj = n // bn
    row = lambda i, j: (i, 0)
    col = lambda i, j: (0, j)
    colb = lambda i, j: (0, nj + j)
    return pl.pallas_call(
        _merge_kernel,
        grid=(m // bm, nj),
        in_specs=[pl.BlockSpec((bm, d), row), pl.BlockSpec((bm, ka), row), pl.BlockSpec((bm, kb), row),
                  pl.BlockSpec((d, bn), col), pl.BlockSpec((d, bn), colb),
                  pl.BlockSpec((1, bn), col), pl.BlockSpec((1, bn), colb),
                  pl.BlockSpec((ka, bn), col), pl.BlockSpec((kb, bn), col)],
        out_specs=pl.BlockSpec((bm, bn), lambda i, j: (i, j)),
        out_shape=jax.ShapeDtypeStruct((m, n), BF16),
        compiler_params=_params(2, 56), name="merge")(xb, oa, ob, wg, wg, bg, bg, wba, wbb)


def _ple_kernel(h_ref, p_ref, wg_ref, bg_ref, wp_ref, o_ref):
    g = jax.nn.sigmoid(_dot(h_ref[...], wg_ref[...]) + bg_ref[...])
    o_ref[...] = (g * _dot(p_ref[...], wp_ref[...])).astype(o_ref.dtype)


def _ple(hb, pb, wg, bg, wp, bm, bn):
    m, d = hb.shape
    dp = pb.shape[1]
    n = wg.shape[1]
    bm, bn = min(bm, m), min(bn, n)
    return pl.pallas_call(
        _ple_kernel,
        grid=(n // bn, m // bm),
        in_specs=[pl.BlockSpec((bm, d), lambda j, i: (i, 0)), pl.BlockSpec((bm, dp), lambda j, i: (i, 0)),
                  pl.BlockSpec((d, bn), lambda j, i: (0, j)), pl.BlockSpec((1, bn), lambda j, i: (0, j)),
                  pl.BlockSpec((dp, bn), lambda j, i: (0, j))],
        out_specs=pl.BlockSpec((bm, bn), lambda j, i: (i, j)),
        out_shape=jax.ShapeDtypeStruct((m, n), BF16),
        compiler_params=_params(2, 56), name="ple")(hb, pb, wg, bg, wp)


def _layer_norm_rows(v, g, b):
    mu = jnp.mean(v, axis=-1, keepdims=True)
    d = v - mu
    var = jnp.mean(d * d, axis=-1, keepdims=True)
    return d * lax.rsqrt(var + LN_EPS) * g + b


def _ln1_kernel(hp_ref, g_ref, b_ref, h_ref, hb_ref, ht_ref):
    y = _layer_norm_rows(hp_ref[...], g_ref[...], b_ref[...])
    h_ref[...] = y
    hb_ref[...] = y.astype(BF16)
    ht_ref[...] = y.T.astype(BF16)


def _ln1(hp, g, b, bm):
    m, d = hp.shape
    bm = min(bm, m)
    return pl.pallas_call(
        _ln1_kernel,
        grid=(m // bm,),
        in_specs=[pl.BlockSpec((bm, d), lambda i: (i, 0)),
                  pl.BlockSpec((1, d), lambda i: (0, 0)), pl.BlockSpec((1, d), lambda i: (0, 0))],
        out_specs=[pl.BlockSpec((bm, d), lambda i: (i, 0)), pl.BlockSpec((bm, d), lambda i: (i, 0)),
                   pl.BlockSpec((d, bm), lambda i: (0, i))],
        out_shape=[jax.ShapeDtypeStruct((m, d), F32), jax.ShapeDtypeStruct((m, d), BF16),
                   jax.ShapeDtypeStruct((d, m), BF16)],
        compiler_params=_params(1, 48), name="ln1")(hp, g, b)


def _ln2_kernel(h_ref, yt_ref, ple_ref, g_ref, b_ref, o_ref, *, alpha):
    v = alpha * h_ref[...] + yt_ref[...].T + ple_ref[...].astype(F32)
    o_ref[...] = _layer_norm_rows(v, g_ref[...], b_ref[...])


def _ln2(h, yt, ple, g, b, alpha, bm):
    m, d = h.shape
    bm = min(bm, m)
    return pl.pallas_call(
        functools.partial(_ln2_kernel, alpha=alpha),
        grid=(m // bm,),
        in_specs=[pl.BlockSpec((bm, d), lambda i: (i, 0)), pl.BlockSpec((d, bm), lambda i: (0, i)),
                  pl.BlockSpec((bm, d), lambda i: (i, 0)),
                  pl.BlockSpec((1, d), lambda i: (0, 0)), pl.BlockSpec((1, d), lambda i: (0, 0))],
        out_specs=pl.BlockSpec((bm, d), lambda i: (i, 0)),
        out_shape=jax.ShapeDtypeStruct((m, d), F32),
        compiler_params=_params(1, 48), name="ln2")(h, yt, ple, g, b)


def _prep_kernel(sm_ref, gl_ref, bf_ref, ckv_ref, kk_ref, w_ref, cum_ref, cumt_ref, *, idx_scale, blk):
    s = sm_ref.shape[0]
    c = sm_ref[:, 0:A_LATENT]
    ms = jnp.mean(c * c, axis=-1, keepdims=True)
    ckv_ref[...] = (c * lax.rsqrt(ms + LN_EPS) * gl_ref[...]).astype(BF16)
    o = A_LATENT
    kk_ref[...] = sm_ref[:, o:o + LANES].astype(BF16)
    w_ref[...] = sm_ref[:, o + LANES:o + 2 * LANES] * idx_scale
    f = sm_ref[:, o + 2 * LANES:o + 3 * LANES] + bf_ref[...]
    ls = jnp.minimum(f, 0.0) - jnp.log1p(jnp.exp(-jnp.abs(f)))
    r = lax.broadcasted_iota(I32, (blk, blk), 0)
    cc = lax.broadcasted_iota(I32, (blk, blk), 1)
    tri = jnp.where(r >= cc, 1.0, 0.0).astype(BF16)
    carry = jnp.zeros((1, LANES), F32)
    for kb in range(s // blk):
        xk = ls[kb * blk:(kb + 1) * blk]
        hi = xk.astype(BF16)
        r1 = xk - hi.astype(F32)
        mid = r1.astype(BF16)
        lo = (r1 - mid.astype(F32)).astype(BF16)
        ck = _dot(tri, hi) + _dot(tri, mid) + _dot(tri, lo) + carry
        cum_ref[kb * blk:(kb + 1) * blk, :] = ck
        carry = ck[blk - 1:blk, :]
    cumt_ref[...] = cum_ref[...].T


def _prep(small, g_latent, b_forget_pad, bsz, seq, idx_scale):
    n, w = small.shape
    blk = min(256, seq)
    return pl.pallas_call(
        functools.partial(_prep_kernel, idx_scale=idx_scale, blk=blk),
        grid=(bsz,),
        in_specs=[pl.BlockSpec((seq, w), lambda b: (b, 0)),
                  pl.BlockSpec((1, A_LATENT), lambda b: (0, 0)), pl.BlockSpec((1, LANES), lambda b: (0, 0))],
        out_specs=[pl.BlockSpec((None, seq, A_LATENT), lambda b: (b, 0, 0)),
                   pl.BlockSpec((None, seq, LANES), lambda b: (b, 0, 0)),
                   pl.BlockSpec((seq, LANES), lambda b: (b, 0)),
                   pl.BlockSpec((seq, LANES), lambda b: (b, 0)),
                   pl.BlockSpec((None, LANES, seq), lambda b: (b, 0, 0))],
        out_shape=[jax.ShapeDtypeStruct((bsz, seq, A_LATENT), BF16),
                   jax.ShapeDtypeStruct((bsz, seq, LANES), BF16),
                   jax.ShapeDtypeStruct((n, LANES), F32),
                   jax.ShapeDtypeStruct((n, LANES), F32),
                   jax.ShapeDtypeStruct((bsz, LANES, seq), F32)],
        compiler_params=_params(1, 48), name="prep")(small, g_latent, b_forget_pad)


FOX_HEADS_PER_STEP = 2


def _fox_kernel(q_ref, k_ref, v_ref, cumc_ref, cumr_ref, o_ref, *, tq, scale):
    hp = pl.program_id(1)
    i = pl.program_id(2)
    nh = FOX_HEADS_PER_STEP
    lane = lax.broadcasted_iota(I32, (tq, LANES), 1)
    qs = [(q_ref[e].astype(F32) * scale).astype(BF16) for e in range(nh)]
    cqs = [jnp.sum(jnp.where(lane == hp * nh + e, cumc_ref[...], 0.0), axis=1, keepdims=True)
           for e in range(nh)]

    def block(j, carry, diagonal):
        off = pl.multiple_of(j * tq, tq)
        out = []
        for e in range(nh):
            m, l, acc = carry[e]
            k = k_ref[e, pl.ds(off, tq), :]
            v = v_ref[e, pl.ds(off, tq), :]
            s = _dot_nt(qs[e], k) + (cqs[e] - cumr_ref[e, j])
            if diagonal:
                row = lax.broadcasted_iota(I32, (tq, tq), 0)
                col = lax.broadcasted_iota(I32, (tq, tq), 1)
                s = jnp.where(col <= row, s, -jnp.inf)
            m_new = jnp.maximum(m, jnp.max(s, axis=1, keepdims=True))
            p = jnp.exp(s - m_new)
            a = jnp.exp(m - m_new)
            l = a * l + jnp.sum(p, axis=1, keepdims=True)
            acc = a * acc + _dot(p.astype(BF16), v)
            out.append((m_new, l, acc))
        return tuple(out)

    init = (jnp.full((tq, 1), -jnp.inf, F32), jnp.zeros((tq, 1), F32), jnp.zeros((tq, B_HEAD_DIM), F32))
    carry = lax.fori_loop(0, i, functools.partial(block, diagonal=False), (init,) * nh)
    final = block(i, carry, True)
    for e in range(nh):
        _, l, acc = final[e]
        o_ref[:, e * B_HEAD_DIM:(e + 1) * B_HEAD_DIM] = (acc / l).astype(o_ref.dtype)


def _fox(big, cum, cumr, bsz, seq, tq, g_q, g_k, g_v):
    n = big.shape[1]
    tq = min(tq, seq)
    nq = seq // tq
    nh = FOX_HEADS_PER_STEP
    assert g_q % nh == 0 and g_k % nh == 0 and g_v % nh == 0 and B_HEADS % nh == 0
    return pl.pallas_call(
        functools.partial(_fox_kernel, tq=tq, scale=B_HEAD_DIM ** -0.5),
        grid=(bsz, B_HEADS // nh, nq),
        in_specs=[pl.BlockSpec((nh, tq, LANES), lambda b, h, i: (g_q // nh + h, b * nq + i, 0)),
                  pl.BlockSpec((nh, seq, LANES), lambda b, h, i: (g_k // nh + h, b, 0)),
                  pl.BlockSpec((nh, seq, LANES), lambda b, h, i: (g_v // nh + h, b, 0)),
                  pl.BlockSpec((tq, LANES), lambda b, h, i: (b * nq + i, 0)),
                  pl.BlockSpec((None, nh, nq, 1, tq), lambda b, h, i: (b, h, 0, 0, 0))],
        out_specs=pl.BlockSpec((tq, nh * B_HEAD_DIM), lambda b, h, i: (b * nq + i, h)),
        out_shape=jax.ShapeDtypeStruct((n, B_HEADS * B_HEAD_DIM), BF16),
        compiler_params=_params(3, 48), name="fox")(big, big, big, cum, cumr)


def _dsa_kernel(qi_ref, qa_ref, kk_ref, w_ref, ckv_ref, wuk_ref, wuv_ref, o_ref,
                qd_ref, wt_ref, key_ref, bias_ref, jstar_ref, qlat_ref, s_ref, mp_ref, lp_ref, p_ref, acc_ref,
                *, tq, seq, kc, k_sel):
    i = pl.program_id(1)
    t0 = i * tq
    nk = (t0 + tq + kc - 1) // kc
    npair = IDX_HEADS // 2
    rows_all = A_HEADS * tq
    slopes = [2.0 ** (-8.0 * (h + 1) / A_HEADS) for h in range(A_HEADS)]

    lane = lax.broadcasted_iota(I32, (tq, LANES), 1)
    for j in range(npair):
        a = qi_ref[j].astype(F32)
        qd_ref[j, 0:tq, :] = jnp.where(lane < IDX_DIM, a, 0.0).astype(BF16)
        qd_ref[j, tq:2 * tq, :] = jnp.where(lane >= IDX_DIM, a, 0.0).astype(BF16)
    wt_ref[...] = w_ref[...].T

    qpos = t0 + lax.broadcasted_iota(I32, (1, tq), 1)
    chunk_end = (qpos // CHUNK + 1) * CHUNK

    def idx_chunk(c, carry):
        kkc = kk_ref[c]
        acc = jnp.zeros((kc, tq), F32)
        for j in range(npair):
            out = _dot_nt(kkc, qd_ref[j])
            acc = acc + jnp.maximum(out[:, :tq], 0.0) * wt_ref[2 * j:2 * j + 1, :]
            acc = acc + jnp.maximum(out[:, tq:], 0.0) * wt_ref[2 * j + 1:2 * j + 2, :]
        kidx = c * kc + lax.broadcasted_iota(I32, (kc, tq), 0)
        bits = pltpu.bitcast(acc, I32)
        key = bits ^ ((bits >> 31) & 0x7FFFFFFF)
        key_ref[c] = jnp.where(kidx < chunk_end, key, INT_MIN)
        return carry

    lax.fori_loop(0, nk, idx_chunk, 0)

    sub = lax.broadcasted_iota(I32, (SUBLANES, tq), 0)

    def count(pred):
        nacc = 4

        def body(c, accs):
            kch = key_ref[c]
            accs = list(accs)
            for g in range(kc // SUBLANES):
                kg = kch[g * SUBLANES:(g + 1) * SUBLANES, :]
                accs[g % nacc] = accs[g % nacc] + jnp.where(pred(kg, c * kc + g * SUBLANES + sub), 1.0, 0.0)
            return tuple(accs)
        accs = lax.fori_loop(0, nk, body, (jnp.zeros((SUBLANES, tq), F32),) * nacc)
        acc = (accs[0] + accs[1]) + (accs[2] + accs[3])
        return jnp.broadcast_to(jnp.sum(acc, axis=0, keepdims=True), (SUBLANES, tq))

    def bis(it, tu):
        cand_u = tu | lax.shift_left(jnp.int32(1), 31 - it)
        cand_s = cand_u ^ INT_MIN
        cnt = count(lambda k, _: k >= cand_s)
        return jnp.where(cnt >= k_sel, cand_u, tu)

    tu = lax.fori_loop(0, 32, bis, jnp.zeros((SUBLANES, tq), I32))
    thr = tu ^ INT_MIN
    cnt_ge = count(lambda k, _: k >= thr)
    tie = jnp.where((cnt_ge > k_sel) & (thr != INT_MIN), 1, 0)
    jstar_ref[...] = jnp.full((SUBLANES, tq), seq, I32)

    @pl.when(jnp.max(tie) > 0)
    def _():
        need = k_sel - count(lambda k, _: k > thr)
        nbits = seq.bit_length()

        def bis2(it, jj):
            cand = jj | lax.shift_left(jnp.int32(1), nbits - 1 - it)
            f = count(lambda k, kidx: (k == thr) & (kidx < cand))
            return jnp.where((cand <= seq) & (f <= need), cand, jj)

        jstar_ref[...] = lax.fori_loop(0, nbits, bis2, jnp.zeros((SUBLANES, tq), I32))

    thr1 = thr[0:1, :]
    jst1 = jstar_ref[0:1, :]

    def bias_chunk(c, carry):
        kch = key_ref[c]
        kidx = c * kc + lax.broadcasted_iota(I32, (kc, tq), 0)
        sel = ((kch > thr1) | ((kch == thr1) & (kidx < jst1))) & (kidx < chunk_end)
        bias_ref[c] = jnp.where(sel, 0.0, -jnp.inf).T
        return carry

    lax.fori_loop(0, nk, bias_chunk, 0)

    for h in range(A_HEADS):
        qlat_ref[h * tq:(h + 1) * tq, :] = _dot(qa_ref[h], wuk_ref[h]).astype(BF16)
    mp_ref[...] = jnp.full((rows_all, LANES), -jnp.inf, F32)
    rowq = t0 + lax.broadcasted_iota(I32, (tq, kc), 0)
    colk = lax.broadcasted_iota(I32, (tq, kc), 1)

    def pass1(c, carry):
        s = _dot_nt(qlat_ref[...], ckv_ref[c])
        b = bias_ref[c]
        dist = jnp.abs(rowq - (colk + c * kc)).astype(F32)
        for h in range(A_HEADS):
            r = slice(h * tq, (h + 1) * tq)
            sh = s[r] - slopes[h] * dist + b
            s_ref[c, r, :] = sh
            mp = mp_ref[r, :]
            for u in range(kc // LANES):
                mp = jnp.maximum(mp, sh[:, u * LANES:(u + 1) * LANES])
            mp_ref[r, :] = mp
        return carry

    lax.fori_loop(0, nk, pass1, 0)
    m = jnp.max(mp_ref[...], axis=1, keepdims=True)
    mp_ref[...] = jnp.broadcast_to(m, (rows_all, LANES))
    lp_ref[...] = jnp.zeros((rows_all, LANES), F32)
    acc_ref[...] = jnp.zeros((rows_all, A_LATENT), F32)

    def pass2(c, carry):
        for h in range(A_HEADS):
            r = slice(h * tq, (h + 1) * tq)
            mrep = mp_ref[r, :]
            lp = lp_ref[r, :]
            for u in range(kc // LANES):
                p = jnp.exp(s_ref[c, r, u * LANES:(u + 1) * LANES] - mrep)
                lp = lp + p
                p_ref[r, u * LANES:(u + 1) * LANES] = p.astype(BF16)
            lp_ref[r, :] = lp
        acc_ref[...] += _dot(p_ref[...], ckv_ref[c])
        return carry

    lax.fori_loop(0, nk, pass2, 0)
    l = jnp.sum(lp_ref[...], axis=1, keepdims=True)
    olat = (acc_ref[...] / l).astype(BF16)
    for h in range(A_HEADS):
        o_ref[:, h * A_HEAD_DIM:(h + 1) * A_HEAD_DIM] = _dot(
            olat[h * tq:(h + 1) * tq], wuv_ref[h]).astype(o_ref.dtype)


def _dsa(big_i, big_a, kk, widx, ckv, wuk_t, wuv_h, bsz, seq):
    n = big_a.shape[1]
    tq = LANES
    assert seq % tq == 0
    kc = min(256, seq)
    nq, nkc = seq // tq, seq // kc
    k_sel = min(TOPK_MAX, seq // 4)
    npair = IDX_HEADS // 2
    rows_all = A_HEADS * tq
    kk = kk.reshape(bsz, nkc, kc, LANES)
    ckv = ckv.reshape(bsz, nkc, kc, A_LATENT)
    return pl.pallas_call(
        functools.partial(_dsa_kernel, tq=tq, seq=seq, kc=kc, k_sel=k_sel),
        grid=(bsz, nq),
        in_specs=[pl.BlockSpec((npair, tq, LANES), lambda b, i: (0, b * nq + i, 0)),
                  pl.BlockSpec((A_HEADS, tq, LANES), lambda b, i: (0, b * nq + i, 0)),
                  pl.BlockSpec((None, nkc, kc, LANES), lambda b, i: (b, 0, 0, 0)),
                  pl.BlockSpec((tq, LANES), lambda b, i: (b * nq + i, 0)),
                  pl.BlockSpec((None, nkc, kc, A_LATENT), lambda b, i: (b, 0, 0, 0)),
                  pl.BlockSpec((A_HEADS, A_HEAD_DIM, A_LATENT), lambda b, i: (0, 0, 0)),
                  pl.BlockSpec((A_HEADS, A_LATENT, A_HEAD_DIM), lambda b, i: (0, 0, 0))],
        out_specs=pl.BlockSpec((tq, A_HEADS * A_HEAD_DIM), lambda b, i: (b * nq + i, 0)),
        out_shape=jax.ShapeDtypeStruct((n, A_HEADS * A_HEAD_DIM), BF16),
        scratch_shapes=[pltpu.VMEM((npair, 2 * tq, LANES), BF16),
                        pltpu.VMEM((LANES, tq), F32),
                        pltpu.VMEM((nkc, kc, tq), I32),
                        pltpu.VMEM((nkc, tq, kc), F32),
                        pltpu.VMEM((SUBLANES, tq), I32),
                        pltpu.VMEM((rows_all, A_LATENT), BF16),
                        pltpu.VMEM((nkc, rows_all, kc), F32),
                        pltpu.VMEM((rows_all, LANES), F32),
                        pltpu.VMEM((rows_all, LANES), F32),
                        pltpu.VMEM((rows_all, kc), BF16),
                        pltpu.VMEM((rows_all, A_LATENT), F32)],
        compiler_params=_params(2, 56), name="dsa")(big_i, big_a, kk, widx, ckv, wuk_t, wuv_h)


def _topk_desc(x, k):
    out = []
    cur = x
    for _ in range(k):
        m = jnp.max(cur, axis=0, keepdims=True)
        out.append(m)
        cur = jnp.where(cur == m, -jnp.inf, cur)
    return out


def _route_kernel(qp_ref, sk_ref, t1_ref, e1_ref, s2_ref, e2_ref, v2_ref, cand_ref, *, heads, rows):
    keys = s2_ref.shape[1]
    k1 = PEER_TOPK + 1
    for h in range(heads):
        s1 = _dot_nt(sk_ref[h, 0], qp_ref[2 * h])
        s2 = _dot_nt(sk_ref[h, 1], qp_ref[2 * h + 1])
        v1 = _topk_desc(s1, k1)
        v2 = _topk_desc(s2, k1)
        for a in range(k1):
            v2_ref[a:a + 1, :] = v2[a]
        off = 0
        for a in range(k1):
            nb = k1 // (a + 1)
            cand_ref[off:off + nb, :] = v1[a] + v2_ref[0:nb, :]
            off += nb
        cand_ref[off:, :] = jnp.full((cand_ref.shape[0] - off, cand_ref.shape[1]), -jnp.inf, F32)
        cand = cand_ref[...]
        tops = _topk_desc(cand, k1)
        t_k = tops[PEER_TOPK - 1]
        t_mid = 0.5 * (t_k + tops[PEER_TOPK])
        mx = v1[0] + v2[0]
        z = jnp.sum(jnp.where(cand >= t_k, jnp.exp(cand - mx), 0.0), axis=0, keepdims=True)
        e1 = jnp.exp(s1 - v1[0]) / z
        t1 = t_mid - s1
        for blk in range(keys // rows):
            t1_ref[blk, h * rows:(h + 1) * rows, :] = t1[blk * rows:(blk + 1) * rows, :]
            e1_ref[blk, h * rows:(h + 1) * rows, :] = e1[blk * rows:(blk + 1) * rows, :]
        s2_ref[h] = s2
        e2_ref[h] = jnp.exp(s2 - v2[0])


def _route(qp, sk, tn, rows):
    g, n, _ = qp.shape
    heads = g // 2
    keys = sk.shape[2]
    tn = min(tn, n)
    nblk = keys // rows
    k1 = PEER_TOPK + 1
    ncand = sum(k1 // (a + 1) for a in range(k1))
    ncand = -(-ncand // SUBLANES) * SUBLANES
    hk = pl.BlockSpec((heads, keys, tn), lambda i: (0, 0, i))
    rk = pl.BlockSpec((nblk, heads * rows, tn), lambda i: (0, 0, i))
    sd = jax.ShapeDtypeStruct((heads, keys, n), F32)
    rd = jax.ShapeDtypeStruct((nblk, heads * rows, n), F32)
    return pl.pallas_call(
        functools.partial(_route_kernel, heads=heads, rows=rows),
        grid=(n // tn,),
        in_specs=[pl.BlockSpec((g, tn, LANES), lambda i: (0, i, 0)),
                  pl.BlockSpec(sk.shape, lambda i: (0, 0, 0, 0))],
        out_specs=[rk, rk, hk, hk],
        out_shape=[rd, rd, sd, sd],
        scratch_shapes=[pltpu.VMEM((-(-k1 // SUBLANES) * SUBLANES, tn), F32), pltpu.VMEM((ncand, tn), F32)],
        compiler_params=_params(1, 48), name="route")(qp, sk)


def _expert_kernel(ht_ref, u_ref, vt_ref, t1_ref, e1_ref, s2_ref, e2_ref, y_ref,
                   act_ref, a_ref, *, heads, rows, ne, dchunk, sb):
    t = pl.program_id(0)
    keys = s2_ref.shape[1]
    d, tn = y_ref.shape
    e_blk = act_ref.shape[1]
    act_new_ref, act_use_ref, a_new_ref, a_use_ref = act_ref.at[0], act_ref.at[1], a_ref.at[0], a_ref.at[1]

    @pl.when(t == 0)
    def _():
        act_new_ref[...] = jnp.zeros(act_new_ref.shape, F32)
        a_new_ref[...] = jnp.zeros(a_new_ref.shape, BF16)
        y_ref[...] = jnp.zeros(y_ref.shape, F32)

    a_use_ref[...] = a_new_ref[...]
    act_use_ref[...] = act_new_ref[...]

    @pl.when((t - 2) % ne == 0)
    def _():
        y_ref[...] = jnp.zeros(y_ref.shape, F32)

    def second_matmul(c, zero):
        rs = slice(c * dchunk, (c + 1) * dchunk)
        y_ref[rs, :] = y_ref[rs, :] + _dot(vt_ref[rs, :], a_use_ref[...])
        c0 = c * dchunk
        y_ref[c0:c0 + SUBLANES, 0:LANES] = y_ref[c0:c0 + SUBLANES, 0:LANES] + zero

    def gate(piece):
        lo = piece * sb
        r, q = lo // keys, lo % keys
        a = act_use_ref[lo:lo + sb, :]
        w = jnp.zeros_like(a)
        for h in range(heads):
            sel = s2_ref[h, q:q + sb, :] >= t1_ref[h * rows + r:h * rows + r + 1, :]
            g = e1_ref[h * rows + r:h * rows + r + 1, :] * e2_ref[h, q:q + sb, :]
            w = w + jnp.where(sel, g, 0.0)
        gelu = 0.5 * a * (1.0 + lax.erf(a * (2.0 ** -0.5)))
        out = w * gelu
        a_new_ref[lo:lo + sb, :] = out.astype(BF16)
        tile = out[0:SUBLANES, 0:LANES]
        for i in range(sb // SUBLANES):
            for j in range(tn // LANES):
                if i or j:
                    tile = tile + out[i * SUBLANES:(i + 1) * SUBLANES, j * LANES:(j + 1) * LANES]
        return tile

    def first_matmul(half):
        c0 = half * (tn // 2)
        act_new_ref[:, c0:c0 + tn // 2] = _dot(u_ref[...], ht_ref[:, c0:c0 + tn // 2])

    def gates_then_zero(pieces):
        tile = gate(pieces[0])
        for piece in pieces[1:]:
            tile = tile + gate(piece)
        bits = pltpu.bitcast(tile, jnp.uint32)
        return pltpu.bitcast((bits >> 16) >> 16, F32)

    nd, npieces = d // dchunk, e_blk // sb
    per = npieces // nd
    for c in range(nd):
        if c == 0:
            first_matmul(0)
        if c == nd // 2:
            first_matmul(1)
        second_matmul(c, gates_then_zero(list(range(c * per, (c + 1) * per))))


def _experts(ht, u, vt, t1r, e1r, s2, e2, tn, e_blk):
    d, n = ht.shape
    heads, keys, _ = s2.shape
    rows = e_blk // keys
    tn = min(tn, n)
    ne = u.shape[0] // e_blk
    last = (n // tn) * ne - 1
    st1 = lambda t: jnp.minimum(t, last)
    st2 = lambda t: jnp.clip(t - 1, 0, last)
    st3 = lambda t: jnp.clip(t - 2, 0, last)
    return pl.pallas_call(
        functools.partial(_expert_kernel, heads=heads, rows=rows, ne=ne, dchunk=min(256, d), sb=32),
        grid=(last + 3,),
        in_specs=[pl.BlockSpec((d, tn), lambda t: (0, st1(t) // ne)),
                  pl.BlockSpec((e_blk, d), lambda t: (st1(t) % ne, 0)),
                  pl.BlockSpec((d, e_blk), lambda t: (0, st3(t) % ne)),
                  pl.BlockSpec((None, heads * rows, tn), lambda t: (st2(t) % ne, 0, st2(t) // ne)),
                  pl.BlockSpec((None, heads * rows, tn), lambda t: (st2(t) % ne, 0, st2(t) // ne)),
                  pl.BlockSpec((heads, keys, tn), lambda t: (0, 0, st2(t) // ne)),
                  pl.BlockSpec((heads, keys, tn), lambda t: (0, 0, st2(t) // ne))],
        out_specs=pl.BlockSpec((d, tn), lambda t: (0, st3(t) // ne)),
        out_shape=jax.ShapeDtypeStruct((d, n), F32),
        scratch_shapes=[pltpu.VMEM((2, e_blk, tn), F32), pltpu.VMEM((2, e_blk, tn), BF16)],
        compiler_params=_params(1, 60), name="experts")(ht, u, vt, t1r, e1r, s2, e2)


def _layer(h, hb, p_i, w_in, b_forget, g_latent, w_uk, w_uv, w_branch_a, w_branch_b, w_gate, b_gate, w_out,
           ln1_g, ln1_b, peer_wq, peer_subkeys, peer_u, peer_v, w_ple, w_ple_gate, b_ple_gate, ln2_g, ln2_b,
           bsz, seq, alpha):
    n, d = h.shape
    wa, wl, wi = A_HEADS * A_HEAD_DIM, A_LATENT, IDX_HEADS * IDX_DIM
    wb = B_HEADS * B_HEAD_DIM
    o = [0, wa, wa + wl, wa + wl + wi, wa + wl + wi + IDX_DIM, wa + wl + wi + IDX_DIM + IDX_HEADS]
    o += [o[-1] + wb, o[-1] + 2 * wb, o[-1] + 3 * wb, o[-1] + 3 * wb + B_HEADS]
    seg = [w_in[:, o[k]:o[k + 1]] for k in range(9)]
    (w_qa, w_ckv, w_qi, w_ki, w_wi, w_qb, w_kb, w_vb, w_f) = seg
    zeros = lambda c: jnp.zeros((d, c), w_in.dtype)
    w_small = jnp.concatenate(
        [w_ckv, w_ki, w_ki, w_wi, zeros(LANES - IDX_HEADS), w_f, zeros(LANES - B_HEADS)], axis=1).astype(BF16)
    w_fox = w_in[:, o[5]:o[8]].astype(BF16)

    big_a = _matmul(hb, w_qa.astype(BF16), BF16, 1024, 1024, head_major=True, name="proj_qa")
    big_i = _matmul(hb, w_qi.astype(BF16), BF16, 1024, 1024, head_major=True, name="proj_qidx")
    big_b = _matmul(hb, w_fox, BF16, 1024, 1024, head_major=True, name="proj_fox")
    small = _matmul(hb, w_small, F32, 1024, w_small.shape[1], name="proj_small")

    bf_pad = jnp.zeros((1, LANES), F32).at[0, :B_HEADS].set(b_forget)
    idx_scale = (IDX_DIM ** -0.5) * (IDX_HEADS ** -0.5)
    ckv, kk, widx, cum, cumt = _prep(small, g_latent.reshape(1, -1), bf_pad, bsz, seq, idx_scale)

    tq = min(512, seq)
    cumr = cumt[:, :B_HEADS, :].reshape(bsz, B_HEADS, seq // tq, 1, tq)
    o_b = _fox(big_b, cum, cumr, bsz, seq, tq, 0, wb // LANES, 2 * wb // LANES)

    wuk_t = (jnp.transpose(w_uk, (1, 2, 0)) * (A_HEAD_DIM ** -0.5)).astype(BF16)
    wuv_h = jnp.transpose(w_uv, (1, 0, 2)).astype(BF16)
    o_a = _dsa(big_i, big_a, kk, widx, ckv, wuk_t, wuv_h, bsz, seq)

    merged = _merge(hb, o_a, o_b, w_gate.astype(BF16), b_gate.reshape(1, 2 * d),
                    w_branch_a.astype(BF16), w_branch_b.astype(BF16), 1024, 256)
    hpre = _matmul_residual(merged, w_out.astype(BF16), h, alpha, 1024, 1024)
    h1, h1b, h1t = _ln1(hpre, ln1_g.reshape(1, d), ln1_b.reshape(1, d), 256)

    heads, qd = peer_wq.shape[1], peer_wq.shape[2]
    keys = peer_subkeys.shape[2]
    qp = _matmul(h1b, peer_wq.reshape(d, heads * qd).astype(BF16), BF16, 1024, 1024, head_major=True,
                 name="peer_q")
    tn = min(512, n)
    e_blk = 512
    t1r, e1r, s2, e2 = _route(qp, peer_subkeys.astype(BF16), LANES, e_blk // keys)
    yt = _experts(h1t, peer_u.astype(BF16), peer_v.T.astype(BF16), t1r, e1r, s2, e2, tn, e_blk)

    ple = _ple(h1b, p_i.astype(BF16), w_ple_gate.astype(BF16), b_ple_gate.reshape(1, d), w_ple.astype(BF16),
               1024, 1024)
    return _ln2(h1, yt, ple, ln2_g.reshape(1, d), ln2_b.reshape(1, d), alpha, 256)


def kernel(x, p, w_in, b_forget, g_latent, w_uk, w_uv, w_branch_a, w_branch_b, w_gate, b_gate, w_out, ln1_g,
           ln1_b, peer_wq, peer_subkeys, peer_u, peer_v, w_ple, w_ple_gate, b_ple_gate, ln2_g, ln2_b):
    bsz, seq, d = x.shape
    depth = w_in.shape[0]
    alpha = (2.0 * depth) ** 0.25
    h = x.reshape(bsz * seq, d)
    for i in range(depth):
        h = _layer(h, h.astype(BF16), p[i].reshape(bsz * seq, -1), w_in[i], b_forget[i], g_latent[i], w_uk[i],
                   w_uv[i], w_branch_a[i], w_branch_b[i], w_gate[i], b_gate[i], w_out[i], ln1_g[i], ln1_b[i],
                   peer_wq[i], peer_subkeys[i], peer_u[i], peer_v[i], w_ple[i], w_ple_gate[i], b_ple_gate[i],
                   ln2_g[i], ln2_b[i], bsz, seq, alpha)
    return h.reshape(bsz, seq, d)
```

```python
import functools

import jax
import jax.numpy as jnp
from jax import lax
from jax.experimental import pallas as pl
from jax.experimental.pallas import tpu as pltpu

F32 = jnp.float32
BF16 = jnp.bfloat16
I32 = jnp.int32

LANES = 128
SUBLANES = 8
CHUNK = 64
A_HEADS = 16
A_HEAD_DIM = 128
A_LATENT = 256
IDX_HEADS = 32
IDX_DIM = 64
TOPK_MAX = 256
B_HEADS = 16
B_HEAD_DIM = 128
PEER_TOPK = 16
LN_EPS = 1e-5
INT_MIN = -2 ** 31
MIB = 1024 * 1024


def _params(n_grid, vmem_mib):
    return pltpu.CompilerParams(dimension_semantics=("arbitrary",) * n_grid,
                                vmem_limit_bytes=vmem_mib * MIB)


def _dot(a, b):
    return jnp.dot(a, b, preferred_element_type=F32)


def _dot_nt(a, b):
    return lax.dot_general(a, b, (((1,), (1,)), ((), ())), preferred_element_type=F32)


def _mm_kernel(a_ref, b_ref, o_ref, *, groups):
    acc = _dot(a_ref[...], b_ref[...])
    if groups is None:
        o_ref[...] = acc.astype(o_ref.dtype)
    else:
        for g in range(groups):
            o_ref[g] = acc[:, g * LANES:(g + 1) * LANES].astype(o_ref.dtype)


def _matmul(a, b, out_dtype, bm, bn, head_major=False, name="mm"):
    m, k = a.shape
    n = b.shape[1]
    bm, bn = min(bm, m), min(bn, n)
    assert m % bm == 0 and n % bn == 0
    if head_major:
        out_shape = jax.ShapeDtypeStruct((n // LANES, m, LANES), out_dtype)
        out_spec = pl.BlockSpec((bn // LANES, bm, LANES), lambda j, i: (j, i, 0))
        groups = bn // LANES
    else:
        out_shape = jax.ShapeDtypeStruct((m, n), out_dtype)
        out_spec = pl.BlockSpec((bm, bn), lambda j, i: (i, j))
        groups = None
    return pl.pallas_call(
        functools.partial(_mm_kernel, groups=groups),
        grid=(n // bn, m // bm),
        in_specs=[pl.BlockSpec((bm, k), lambda j, i: (i, 0)),
                  pl.BlockSpec((k, bn), lambda j, i: (0, j))],
        out_specs=out_spec, out_shape=out_shape,
        compiler_params=_params(2, 56), name=name)(a, b)


def _mm_res_kernel(a_ref, b_ref, x_ref, o_ref, *, alpha):
    o_ref[...] = alpha * x_ref[...] + _dot(a_ref[...], b_ref[...])


def _matmul_residual(a, b, x, alpha, bm, bn):
    m, k = a.shape
    n = b.shape[1]
    bm, bn = min(bm, m), min(bn, n)
    return pl.pallas_call(
        functools.partial(_mm_res_kernel, alpha=alpha),
        grid=(n // bn, m // bm),
        in_specs=[pl.BlockSpec((bm, k), lambda j, i: (i, 0)),
                  pl.BlockSpec((k, bn), lambda j, i: (0, j)),
                  pl.BlockSpec((bm, bn), lambda j, i: (i, j))],
        out_specs=pl.BlockSpec((bm, bn), lambda j, i: (i, j)),
        out_shape=jax.ShapeDtypeStruct((m, n), F32),
        compiler_params=_params(2, 56), name="outproj")(a, b, x)


def _merge_kernel(x_ref, oa_ref, ob_ref, wga_ref, wgb_ref, bga_ref, bgb_ref, wba_ref, wbb_ref, o_ref):
    x = x_ref[...]
    ga = jax.nn.sigmoid(_dot(x, wga_ref[...]) + bga_ref[...])
    gb = jax.nn.sigmoid(_dot(x, wgb_ref[...]) + bgb_ref[...])
    ya = _dot(oa_ref[...], wba_ref[...])
    yb = _dot(ob_ref[...], wbb_ref[...])
    o_ref[...] = (ga * ya + gb * yb).astype(o_ref.dtype)


def _merge(xb, oa, ob, wg, bg, wba, wbb, bm, bn):
    m, d = xb.shape
    ka, kb = oa.shape[1], ob.shape[1]
    n = wba.shape[1]
    bm, bn = min(bm, m), min(bn, n)
    nj = n // bn
    row = lambda i, j: (i, 0)
    col = lambda i, j: (0, j)
    colb = lambda i, j: (0, nj + j)
    return pl.pallas_call(
        _merge_kernel,
        grid=(m // bm, nj),
        in_specs=[pl.BlockSpec((bm, d), row), pl.BlockSpec((bm, ka), row), pl.BlockSpec((bm, kb), row),
                  pl.BlockSpec((d, bn), col), pl.BlockSpec((d, bn), colb),
                  pl.BlockSpec((1, bn), col), pl.BlockSpec((1, bn), colb),
                  pl.BlockSpec((ka, bn), col), pl.BlockSpec((kb, bn), col)],
        out_specs=pl.BlockSpec((bm, bn), lambda i, j: (i, j)),
        out_shape=jax.ShapeDtypeStruct((m, n), BF16),
        compiler_params=_params(2, 56), name="merge")(xb, oa, ob, wg, wg, bg, bg, wba, wbb)


def _ple_kernel(h_ref, p_ref, wg_ref, bg_ref, wp_ref, o_ref):
    g = jax.nn.sigmoid(_dot(h_ref[...], wg_ref[...]) + bg_ref[...])
    o_ref[...] = (g * _dot(p_ref[...], wp_ref[...])).astype(o_ref.dtype)


def _ple(hb, pb, wg, bg, wp, bm, bn):
    m, d = hb.shape
    dp = pb.shape[1]
    n = wg.shape[1]
    bm, bn = min(bm, m), min(bn, n)
    return pl.pallas_call(
        _ple_kernel,
        grid=(n // bn, m // bm),
        in_specs=[pl.BlockSpec((bm, d), lambda j, i: (i, 0)), pl.BlockSpec((bm, dp), lambda j, i: (i, 0)),
                  pl.BlockSpec((d, bn), lambda j, i: (0, j)), pl.BlockSpec((1, bn), lambda j, i: (0, j)),
                  pl.BlockSpec((dp, bn), lambda j, i: (0, j))],
        out_specs=pl.BlockSpec((bm, bn), lambda j, i: (i, j)),
        out_shape=jax.ShapeDtypeStruct((m, n), BF16),
        compiler_params=_params(2, 56), name="ple")(hb, pb, wg, bg, wp)


def _transpose_cast_kernel(x_ref, o_ref):
    o_ref[...] = x_ref[...].T.astype(o_ref.dtype)


def _transpose_cast(x, dtype, br, bc):
    r, c = x.shape
    br, bc = min(br, r), min(bc, c)
    return pl.pallas_call(
        _transpose_cast_kernel,
        grid=(r // br, c // bc),
        in_specs=[pl.BlockSpec((br, bc), lambda i, j: (i, j))],
        out_specs=pl.BlockSpec((bc, br), lambda i, j: (j, i)),
        out_shape=jax.ShapeDtypeStruct((c, r), dtype),
        compiler_params=_params(2, 48), name="transpose_cast")(x)


def _layer_norm_rows(v, g, b):
    mu = jnp.mean(v, axis=-1, keepdims=True)
    d = v - mu
    var = jnp.mean(d * d, axis=-1, keepdims=True)
    return d * lax.rsqrt(var + LN_EPS) * g + b


def _ln1_kernel(hp_ref, g_ref, b_ref, h_ref, hb_ref, ht_ref):
    y = _layer_norm_rows(hp_ref[...], g_ref[...], b_ref[...])
    h_ref[...] = y
    hb_ref[...] = y.astype(BF16)
    ht_ref[...] = y.T.astype(BF16)


def _ln1(hp, g, b, bm):
    m, d = hp.shape
    bm = min(bm, m)
    return pl.pallas_call(
        _ln1_kernel,
        grid=(m // bm,),
        in_specs=[pl.BlockSpec((bm, d), lambda i: (i, 0)),
                  pl.BlockSpec((1, d), lambda i: (0, 0)), pl.BlockSpec((1, d), lambda i: (0, 0))],
        out_specs=[pl.BlockSpec((bm, d), lambda i: (i, 0)), pl.BlockSpec((bm, d), lambda i: (i, 0)),
                   pl.BlockSpec((d, bm), lambda i: (0, i))],
        out_shape=[jax.ShapeDtypeStruct((m, d), F32), jax.ShapeDtypeStruct((m, d), BF16),
                   jax.ShapeDtypeStruct((d, m), BF16)],
        compiler_params=_params(1, 48), name="ln1")(hp, g, b)


def _ln2_kernel(h_ref, yt_ref, ple_ref, g_ref, b_ref, o_ref, *, alpha):
    v = alpha * h_ref[...] + yt_ref[...].astype(F32).T + ple_ref[...].astype(F32)
    o_ref[...] = _layer_norm_rows(v, g_ref[...], b_ref[...])


def _ln2(h, yt, ple, g, b, alpha, bm):
    m, d = h.shape
    bm = min(bm, m)
    return pl.pallas_call(
        functools.partial(_ln2_kernel, alpha=alpha),
        grid=(m // bm,),
        in_specs=[pl.BlockSpec((bm, d), lambda i: (i, 0)), pl.BlockSpec((d, bm), lambda i: (0, i)),
                  pl.BlockSpec((bm, d), lambda i: (i, 0)),
                  pl.BlockSpec((1, d), lambda i: (0, 0)), pl.BlockSpec((1, d), lambda i: (0, 0))],
        out_specs=pl.BlockSpec((bm, d), lambda i: (i, 0)),
        out_shape=jax.ShapeDtypeStruct((m, d), F32),
        compiler_params=_params(1, 48), name="ln2")(h, yt, ple, g, b)


def _prep_kernel(sm_ref, gl_ref, bf_ref, ckv_ref, kk_ref, w_ref, cum_ref, cumt_ref, *, idx_scale, blk):
    s = sm_ref.shape[0]
    c = sm_ref[:, 0:A_LATENT]
    ms = jnp.mean(c * c, axis=-1, keepdims=True)
    ckv_ref[...] = (c * lax.rsqrt(ms + LN_EPS) * gl_ref[...]).astype(BF16)
    o = A_LATENT
    kk_ref[...] = sm_ref[:, o:o + LANES].astype(BF16)
    w_ref[...] = sm_ref[:, o + LANES:o + 2 * LANES] * idx_scale
    f = sm_ref[:, o + 2 * LANES:o + 3 * LANES] + bf_ref[...]
    ls = jnp.minimum(f, 0.0) - jnp.log1p(jnp.exp(-jnp.abs(f)))
    r = lax.broadcasted_iota(I32, (blk, blk), 0)
    cc = lax.broadcasted_iota(I32, (blk, blk), 1)
    tri = jnp.where(r >= cc, 1.0, 0.0).astype(BF16)
    carry = jnp.zeros((1, LANES), F32)
    for kb in range(s // blk):
        xk = ls[kb * blk:(kb + 1) * blk]
        hi = xk.astype(BF16)
        r1 = xk - hi.astype(F32)
        mid = r1.astype(BF16)
        lo = (r1 - mid.astype(F32)).astype(BF16)
        ck = _dot(tri, hi) + _dot(tri, mid) + _dot(tri, lo) + carry
        cum_ref[kb * blk:(kb + 1) * blk, :] = ck
        carry = ck[blk - 1:blk, :]
    cumt_ref[...] = cum_ref[...].T


def _prep(small, g_latent, b_forget_pad, bsz, seq, idx_scale):
    n, w = small.shape
    blk = min(256, seq)
    return pl.pallas_call(
        functools.partial(_prep_kernel, idx_scale=idx_scale, blk=blk),
        grid=(bsz,),
        in_specs=[pl.BlockSpec((seq, w), lambda b: (b, 0)),
                  pl.BlockSpec((1, A_LATENT), lambda b: (0, 0)), pl.BlockSpec((1, LANES), lambda b: (0, 0))],
        out_specs=[pl.BlockSpec((None, seq, A_LATENT), lambda b: (b, 0, 0)),
                   pl.BlockSpec((None, seq, LANES), lambda b: (b, 0, 0)),
                   pl.BlockSpec((seq, LANES), lambda b: (b, 0)),
                   pl.BlockSpec((seq, LANES), lambda b: (b, 0)),
                   pl.BlockSpec((None, LANES, seq), lambda b: (b, 0, 0))],
        out_shape=[jax.ShapeDtypeStruct((bsz, seq, A_LATENT), BF16),
                   jax.ShapeDtypeStruct((bsz, seq, LANES), BF16),
                   jax.ShapeDtypeStruct((n, LANES), F32),
                   jax.ShapeDtypeStruct((n, LANES), F32),
                   jax.ShapeDtypeStruct((bsz, LANES, seq), F32)],
        compiler_params=_params(1, 48), name="prep")(small, g_latent, b_forget_pad)


FOX_HEADS_PER_STEP = 2


def _fox_kernel(q_ref, k_ref, v_ref, cumc_ref, cumr_ref, o_ref, *, tq, scale):
    hp = pl.program_id(1)
    i = pl.program_id(2)
    nh = FOX_HEADS_PER_STEP
    lane = lax.broadcasted_iota(I32, (tq, LANES), 1)
    qs = [(q_ref[e].astype(F32) * scale).astype(BF16) for e in range(nh)]
    cqs = [jnp.sum(jnp.where(lane == hp * nh + e, cumc_ref[...], 0.0), axis=1, keepdims=True)
           for e in range(nh)]

    def block(j, carry, diagonal):
        off = pl.multiple_of(j * tq, tq)
        out = []
        for e in range(nh):
            m, l, acc = carry[e]
            k = k_ref[e, pl.ds(off, tq), :]
            v = v_ref[e, pl.ds(off, tq), :]
            s = _dot_nt(qs[e], k) + (cqs[e] - cumr_ref[e, j])
            if diagonal:
                row = lax.broadcasted_iota(I32, (tq, tq), 0)
                col = lax.broadcasted_iota(I32, (tq, tq), 1)
                s = jnp.where(col <= row, s, -jnp.inf)
            m_new = jnp.maximum(m, jnp.max(s, axis=1, keepdims=True))
            p = jnp.exp(s - m_new)
            a = jnp.exp(m - m_new)
            l = a * l + jnp.sum(p, axis=1, keepdims=True)
            acc = a * acc + _dot(p.astype(BF16), v)
            out.append((m_new, l, acc))
        return tuple(out)

    init = (jnp.full((tq, 1), -jnp.inf, F32), jnp.zeros((tq, 1), F32), jnp.zeros((tq, B_HEAD_DIM), F32))
    carry = lax.fori_loop(0, i, functools.partial(block, diagonal=False), (init,) * nh)
    final = block(i, carry, True)
    for e in range(nh):
        _, l, acc = final[e]
        o_ref[:, e * B_HEAD_DIM:(e + 1) * B_HEAD_DIM] = (acc / l).astype(o_ref.dtype)


def _fox(big, cum, cumr, bsz, seq, tq, g_q, g_k, g_v):
    n = big.shape[1]
    tq = min(tq, seq)
    nq = seq // tq
    nh = FOX_HEADS_PER_STEP
    assert g_q % nh == 0 and g_k % nh == 0 and g_v % nh == 0 and B_HEADS % nh == 0
    return pl.pallas_call(
        functools.partial(_fox_kernel, tq=tq, scale=B_HEAD_DIM ** -0.5),
        grid=(bsz, B_HEADS // nh, nq),
        in_specs=[pl.BlockSpec((nh, tq, LANES), lambda b, h, i: (g_q // nh + h, b * nq + i, 0)),
                  pl.BlockSpec((nh, seq, LANES), lambda b, h, i: (g_k // nh + h, b, 0)),
                  pl.BlockSpec((nh, seq, LANES), lambda b, h, i: (g_v // nh + h, b, 0)),
                  pl.BlockSpec((tq, LANES), lambda b, h, i: (b * nq + i, 0)),
                  pl.BlockSpec((None, nh, nq, 1, tq), lambda b, h, i: (b, h, 0, 0, 0))],
        out_specs=pl.BlockSpec((tq, nh * B_HEAD_DIM), lambda b, h, i: (b * nq + i, h)),
        out_shape=jax.ShapeDtypeStruct((n, B_HEADS * B_HEAD_DIM), BF16),
        compiler_params=_params(3, 48), name="fox")(big, big, big, cum, cumr)


def _dsa_kernel(qi_ref, qa_ref, kk_ref, w_ref, ckv_ref, wuk_ref, wuv_ref, o_ref,
                qd_ref, wt_ref, key_ref, bias_ref, jstar_ref, qlat_ref, s_ref, mp_ref, lp_ref, p_ref, acc_ref,
                *, tq, seq, kc, k_sel):
    i = pl.program_id(1)
    t0 = i * tq
    nk = (t0 + tq + kc - 1) // kc
    npair = IDX_HEADS // 2
    rows_all = A_HEADS * tq
    slopes = [2.0 ** (-8.0 * (h + 1) / A_HEADS) for h in range(A_HEADS)]

    lane = lax.broadcasted_iota(I32, (tq, LANES), 1)
    for j in range(npair):
        a = qi_ref[j].astype(F32)
        qd_ref[j, 0:tq, :] = jnp.where(lane < IDX_DIM, a, 0.0).astype(BF16)
        qd_ref[j, tq:2 * tq, :] = jnp.where(lane >= IDX_DIM, a, 0.0).astype(BF16)
    wt_ref[...] = w_ref[...].T

    qpos = t0 + lax.broadcasted_iota(I32, (1, tq), 1)
    chunk_end = (qpos // CHUNK + 1) * CHUNK

    def idx_chunk(c, carry):
        kkc = kk_ref[c]
        acc = jnp.zeros((kc, tq), F32)
        for j in range(npair):
            out = _dot_nt(kkc, qd_ref[j])
            acc = acc + jnp.maximum(out[:, :tq], 0.0) * wt_ref[2 * j:2 * j + 1, :]
            acc = acc + jnp.maximum(out[:, tq:], 0.0) * wt_ref[2 * j + 1:2 * j + 2, :]
        kidx = c * kc + lax.broadcasted_iota(I32, (kc, tq), 0)
        bits = pltpu.bitcast(acc, I32)
        key = bits ^ ((bits >> 31) & 0x7FFFFFFF)
        key_ref[c] = jnp.where(kidx < chunk_end, key, INT_MIN)
        return carry

    lax.fori_loop(0, nk, idx_chunk, 0)

    sub = lax.broadcasted_iota(I32, (SUBLANES, tq), 0)

    def count(pred):
        nacc = 4

        def body(c, accs):
            kch = key_ref[c]
            accs = list(accs)
            for g in range(kc // SUBLANES):
                kg = kch[g * SUBLANES:(g + 1) * SUBLANES, :]
                accs[g % nacc] = accs[g % nacc] + jnp.where(pred(kg, c * kc + g * SUBLANES + sub), 1.0, 0.0)
            return tuple(accs)
        accs = lax.fori_loop(0, nk, body, (jnp.zeros((SUBLANES, tq), F32),) * nacc)
        acc = (accs[0] + accs[1]) + (accs[2] + accs[3])
        return jnp.broadcast_to(jnp.sum(acc, axis=0, keepdims=True), (SUBLANES, tq))

    def bis(it, tu):
        cand_u = tu | lax.shift_left(jnp.int32(1), 31 - it)
        cand_s = cand_u ^ INT_MIN
        cnt = count(lambda k, _: k >= cand_s)
        return jnp.where(cnt >= k_sel, cand_u, tu)

    tu = lax.fori_loop(0, 32, bis, jnp.zeros((SUBLANES, tq), I32))
    thr = tu ^ INT_MIN
    cnt_ge = count(lambda k, _: k >= thr)
    tie = jnp.where((cnt_ge > k_sel) & (thr != INT_MIN), 1, 0)
    jstar_ref[...] = jnp.full((SUBLANES, tq), seq, I32)

    @pl.when(jnp.max(tie) > 0)
    def _():
        need = k_sel - count(lambda k, _: k > thr)
        nbits = seq.bit_length()

        def bis2(it, jj):
            cand = jj | lax.shift_left(jnp.int32(1), nbits - 1 - it)
            f = count(lambda k, kidx: (k == thr) & (kidx < cand))
            return jnp.where((cand <= seq) & (f <= need), cand, jj)

        jstar_ref[...] = lax.fori_loop(0, nbits, bis2, jnp.zeros((SUBLANES, tq), I32))

    thr1 = thr[0:1, :]
    jst1 = jstar_ref[0:1, :]

    def bias_chunk(c, carry):
        kch = key_ref[c]
        kidx = c * kc + lax.broadcasted_iota(I32, (kc, tq), 0)
        sel = ((kch > thr1) | ((kch == thr1) & (kidx < jst1))) & (kidx < chunk_end)
        bias_ref[c] = jnp.where(sel, 0.0, -jnp.inf).T
        return carry

    lax.fori_loop(0, nk, bias_chunk, 0)

    for h in range(A_HEADS):
        qlat_ref[h * tq:(h + 1) * tq, :] = _dot(qa_ref[h], wuk_ref[h]).astype(BF16)
    mp_ref[...] = jnp.full((rows_all, LANES), -jnp.inf, F32)
    rowq = t0 + lax.broadcasted_iota(I32, (tq, kc), 0)
    colk = lax.broadcasted_iota(I32, (tq, kc), 1)

    def pass1(c, carry):
        s = _dot_nt(qlat_ref[...], ckv_ref[c])
        b = bias_ref[c]
        dist = jnp.abs(rowq - (colk + c * kc)).astype(F32)
        for h in range(A_HEADS):
            r = slice(h * tq, (h + 1) * tq)
            sh = s[r] - slopes[h] * dist + b
            s_ref[c, r, :] = sh
            mp = mp_ref[r, :]
            for u in range(kc // LANES):
                mp = jnp.maximum(mp, sh[:, u * LANES:(u + 1) * LANES])
            mp_ref[r, :] = mp
        return carry

    lax.fori_loop(0, nk, pass1, 0)
    m = jnp.max(mp_ref[...], axis=1, keepdims=True)
    mp_ref[...] = jnp.broadcast_to(m, (rows_all, LANES))
    lp_ref[...] = jnp.zeros((rows_all, LANES), F32)
    acc_ref[...] = jnp.zeros((rows_all, A_LATENT), F32)

    def pass2(c, carry):
        for h in range(A_HEADS):
            r = slice(h * tq, (h + 1) * tq)
            mrep = mp_ref[r, :]
            lp = lp_ref[r, :]
            for u in range(kc // LANES):
                p = jnp.exp(s_ref[c, r, u * LANES:(u + 1) * LANES] - mrep)
                lp = lp + p
                p_ref[r, u * LANES:(u + 1) * LANES] = p.astype(BF16)
            lp_ref[r, :] = lp
        acc_ref[...] += _dot(p_ref[...], ckv_ref[c])
        return carry

    lax.fori_loop(0, nk, pass2, 0)
    l = jnp.sum(lp_ref[...], axis=1, keepdims=True)
    olat = (acc_ref[...] / l).astype(BF16)
    for h in range(A_HEADS):
        o_ref[:, h * A_HEAD_DIM:(h + 1) * A_HEAD_DIM] = _dot(
            olat[h * tq:(h + 1) * tq], wuv_ref[h]).astype(o_ref.dtype)


def _dsa(big_i, big_a, kk, widx, ckv, wuk_t, wuv_h, bsz, seq):
    n = big_a.shape[1]
    tq = LANES
    assert seq % tq == 0
    kc = min(256, seq)
    nq, nkc = seq // tq, seq // kc
    k_sel = min(TOPK_MAX, seq // 4)
    npair = IDX_HEADS // 2
    rows_all = A_HEADS * tq
    kk = kk.reshape(bsz, nkc, kc, LANES)
    ckv = ckv.reshape(bsz, nkc, kc, A_LATENT)
    return pl.pallas_call(
        functools.partial(_dsa_kernel, tq=tq, seq=seq, kc=kc, k_sel=k_sel),
        grid=(bsz, nq),
        in_specs=[pl.BlockSpec((npair, tq, LANES), lambda b, i: (0, b * nq + i, 0)),
                  pl.BlockSpec((A_HEADS, tq, LANES), lambda b, i: (0, b * nq + i, 0)),
                  pl.BlockSpec((None, nkc, kc, LANES), lambda b, i: (b, 0, 0, 0)),
                  pl.BlockSpec((tq, LANES), lambda b, i: (b * nq + i, 0)),
                  pl.BlockSpec((None, nkc, kc, A_LATENT), lambda b, i: (b, 0, 0, 0)),
                  pl.BlockSpec((A_HEADS, A_HEAD_DIM, A_LATENT), lambda b, i: (0, 0, 0)),
                  pl.BlockSpec((A_HEADS, A_LATENT, A_HEAD_DIM), lambda b, i: (0, 0, 0))],
        out_specs=pl.BlockSpec((tq, A_HEADS * A_HEAD_DIM), lambda b, i: (b * nq + i, 0)),
        out_shape=jax.ShapeDtypeStruct((n, A_HEADS * A_HEAD_DIM), BF16),
        scratch_shapes=[pltpu.VMEM((npair, 2 * tq, LANES), BF16),
                        pltpu.VMEM((LANES, tq), F32),
                        pltpu.VMEM((nkc, kc, tq), I32),
                        pltpu.VMEM((nkc, tq, kc), F32),
                        pltpu.VMEM((SUBLANES, tq), I32),
                        pltpu.VMEM((rows_all, A_LATENT), BF16),
                        pltpu.VMEM((nkc, rows_all, kc), F32),
                        pltpu.VMEM((rows_all, LANES), F32),
                        pltpu.VMEM((rows_all, LANES), F32),
                        pltpu.VMEM((rows_all, kc), BF16),
                        pltpu.VMEM((rows_all, A_LATENT), F32)],
        compiler_params=_params(2, 56), name="dsa")(big_i, big_a, kk, widx, ckv, wuk_t, wuv_h)


def _topk_desc(x, k):
    out = []
    cur = x
    for _ in range(k):
        m = jnp.max(cur, axis=0, keepdims=True)
        out.append(m)
        cur = jnp.where(cur == m, -jnp.inf, cur)
    return out


def _route_kernel(qp_ref, sk_ref, t1_ref, e1_ref, s2_ref, e2_ref, v2_ref, cand_ref, *, heads, rows):
    keys = s2_ref.shape[1]
    k1 = PEER_TOPK + 1
    for h in range(heads):
        s1 = _dot_nt(sk_ref[h, 0], qp_ref[2 * h])
        s2 = _dot_nt(sk_ref[h, 1], qp_ref[2 * h + 1])
        v1 = _topk_desc(s1, k1)
        v2 = _topk_desc(s2, k1)
        for a in range(k1):
            v2_ref[a:a + 1, :] = v2[a]
        off = 0
        for a in range(k1):
            nb = k1 // (a + 1)
            cand_ref[off:off + nb, :] = v1[a] + v2_ref[0:nb, :]
            off += nb
        cand_ref[off:, :] = jnp.full((cand_ref.shape[0] - off, cand_ref.shape[1]), -jnp.inf, F32)
        cand = cand_ref[...]
        tops = _topk_desc(cand, k1)
        t_k = tops[PEER_TOPK - 1]
        t_mid = 0.5 * (t_k + tops[PEER_TOPK])
        mx = v1[0] + v2[0]
        z = jnp.sum(jnp.where(cand >= t_k, jnp.exp(cand - mx), 0.0), axis=0, keepdims=True)
        e1 = jnp.exp(s1 - v1[0]) / z
        t1 = t_mid - s1
        for blk in range(keys // rows):
            t1_ref[blk, h * rows:(h + 1) * rows, :] = t1[blk * rows:(blk + 1) * rows, :]
            e1_ref[blk, h * rows:(h + 1) * rows, :] = e1[blk * rows:(blk + 1) * rows, :]
        s2_ref[h] = s2
        e2_ref[h] = jnp.exp(s2 - v2[0])


def _route(qp, sk, tn, rows):
    g, n, _ = qp.shape
    heads = g // 2
    keys = sk.shape[2]
    tn = min(tn, n)
    nblk = keys // rows
    k1 = PEER_TOPK + 1
    ncand = sum(k1 // (a + 1) for a in range(k1))
    ncand = -(-ncand // SUBLANES) * SUBLANES
    hk = pl.BlockSpec((heads, keys, tn), lambda i: (0, 0, i))
    rk = pl.BlockSpec((nblk, heads * rows, tn), lambda i: (0, 0, i))
    sd = jax.ShapeDtypeStruct((heads, keys, n), F32)
    rd = jax.ShapeDtypeStruct((nblk, heads * rows, n), F32)
    return pl.pallas_call(
        functools.partial(_route_kernel, heads=heads, rows=rows),
        grid=(n // tn,),
        in_specs=[pl.BlockSpec((g, tn, LANES), lambda i: (0, i, 0)),
                  pl.BlockSpec(sk.shape, lambda i: (0, 0, 0, 0))],
        out_specs=[rk, rk, hk, hk],
        out_shape=[rd, rd, sd, sd],
        scratch_shapes=[pltpu.VMEM((-(-k1 // SUBLANES) * SUBLANES, tn), F32), pltpu.VMEM((ncand, tn), F32)],
        compiler_params=_params(1, 48), name="route")(qp, sk)


def _expert_kernel(ht_ref, u_ref, vt_ref, t1_ref, e1_ref, s2_ref, e2_ref, y_ref,
                   act_ref, a_ref, yacc_ref, *, heads, rows, ne, dchunk, sb):
    t = pl.program_id(0)
    keys = s2_ref.shape[1]
    d, tn = yacc_ref.shape
    e_blk = act_ref.shape[1]
    act_new_ref, act_use_ref, a_new_ref, a_use_ref = act_ref.at[0], act_ref.at[1], a_ref.at[0], a_ref.at[1]

    @pl.when(t == 0)
    def _():
        act_new_ref[...] = jnp.zeros(act_new_ref.shape, F32)
        a_new_ref[...] = jnp.zeros(a_new_ref.shape, BF16)
        yacc_ref[...] = jnp.zeros(yacc_ref.shape, F32)

    a_use_ref[...] = a_new_ref[...]
    act_use_ref[...] = act_new_ref[...]

    @pl.when((t - 2) % ne == 0)
    def _():
        yacc_ref[...] = jnp.zeros(yacc_ref.shape, F32)

    def second_matmul(c, zero):
        rs = slice(c * dchunk, (c + 1) * dchunk)
        yacc_ref[rs, :] = yacc_ref[rs, :] + _dot(vt_ref[rs, :], a_use_ref[...])
        c0 = c * dchunk
        yacc_ref[c0:c0 + SUBLANES, 0:LANES] = yacc_ref[c0:c0 + SUBLANES, 0:LANES] + zero

    def gate(piece):
        lo = piece * sb
        r, q = lo // keys, lo % keys
        a = act_use_ref[lo:lo + sb, :]
        w = jnp.zeros_like(a)
        for h in range(heads):
            sel = s2_ref[h, q:q + sb, :] >= t1_ref[h * rows + r:h * rows + r + 1, :]
            g = e1_ref[h * rows + r:h * rows + r + 1, :] * e2_ref[h, q:q + sb, :]
            w = w + jnp.where(sel, g, 0.0)
        gelu = 0.5 * a * (1.0 + lax.erf(a * (2.0 ** -0.5)))
        out = w * gelu
        a_new_ref[lo:lo + sb, :] = out.astype(BF16)
        tile = out[0:SUBLANES, 0:LANES]
        for i in range(sb // SUBLANES):
            for j in range(tn // LANES):
                if i or j:
                    tile = tile + out[i * SUBLANES:(i + 1) * SUBLANES, j * LANES:(j + 1) * LANES]
        return tile

    def first_matmul(half):
        c0 = half * (tn // 2)
        act_new_ref[:, c0:c0 + tn // 2] = _dot(u_ref[...], ht_ref[:, c0:c0 + tn // 2])

    def gates_then_zero(pieces):
        tile = gate(pieces[0])
        for piece in pieces[1:]:
            tile = tile + gate(piece)
        bits = pltpu.bitcast(tile, jnp.uint32)
        return pltpu.bitcast((bits >> 16) >> 16, F32)

    nd, npieces = d // dchunk, e_blk // sb
    per = npieces // nd
    for c in range(nd):
        if c == 0:
            first_matmul(0)
        if c == nd // 2:
            first_matmul(1)
        second_matmul(c, gates_then_zero(list(range(c * per, (c + 1) * per))))

    @pl.when((t - 2) % ne == ne - 1)
    def _():
        y_ref[...] = yacc_ref[...].astype(y_ref.dtype)


def _experts(ht, u, vt, t1r, e1r, s2, e2, tn, e_blk):
    d, n = ht.shape
    heads, keys, _ = s2.shape
    rows = e_blk // keys
    tn = min(tn, n)
    ne = u.shape[0] // e_blk
    last = (n // tn) * ne - 1
    st1 = lambda t: jnp.minimum(t, last)
    st2 = lambda t: jnp.clip(t - 1, 0, last)
    st3 = lambda t: jnp.clip(t - 2, 0, last)
    return pl.pallas_call(
        functools.partial(_expert_kernel, heads=heads, rows=rows, ne=ne, dchunk=min(256, d), sb=32),
        grid=(last + 3,),
        in_specs=[pl.BlockSpec((d, tn), lambda t: (0, st1(t) // ne)),
                  pl.BlockSpec((e_blk, d), lambda t: (st1(t) % ne, 0)),
                  pl.BlockSpec((d, e_blk), lambda t: (0, st3(t) % ne)),
                  pl.BlockSpec((None, heads * rows, tn), lambda t: (st2(t) % ne, 0, st2(t) // ne)),
                  pl.BlockSpec((None, heads * rows, tn), lambda t: (st2(t) % ne, 0, st2(t) // ne)),
                  pl.BlockSpec((heads, keys, tn), lambda t: (0, 0, st2(t) // ne)),
                  pl.BlockSpec((heads, keys, tn), lambda t: (0, 0, st2(t) // ne))],
        out_specs=pl.BlockSpec((d, tn), lambda t: (0, st3(t) // ne)),
        out_shape=jax.ShapeDtypeStruct((d, n), BF16),
        scratch_shapes=[pltpu.VMEM((2, e_blk, tn), F32), pltpu.VMEM((2, e_blk, tn), BF16),
                        pltpu.VMEM((d, tn), F32)],
        compiler_params=_params(1, 60), name="experts")(ht, u, vt, t1r, e1r, s2, e2)


def _layer(h, hb, p_i, w_in, b_forget, g_latent, w_uk, w_uv, w_branch_a, w_branch_b, w_gate, b_gate, w_out,
           ln1_g, ln1_b, peer_wq, peer_subkeys, peer_u, peer_v, w_ple, w_ple_gate, b_ple_gate, ln2_g, ln2_b,
           bsz, seq, alpha):
    n, d = h.shape
    wa, wl, wi = A_HEADS * A_HEAD_DIM, A_LATENT, IDX_HEADS * IDX_DIM
    wb = B_HEADS * B_HEAD_DIM
    o = [0, wa, wa + wl, wa + wl + wi, wa + wl + wi + IDX_DIM, wa + wl + wi + IDX_DIM + IDX_HEADS]
    o += [o[-1] + wb, o[-1] + 2 * wb, o[-1] + 3 * wb, o[-1] + 3 * wb + B_HEADS]
    seg = [w_in[:, o[k]:o[k + 1]] for k in range(9)]
    (w_qa, w_ckv, w_qi, w_ki, w_wi, w_qb, w_kb, w_vb, w_f) = seg
    zeros = lambda c: jnp.zeros((d, c), w_in.dtype)
    w_small = jnp.concatenate(
        [w_ckv, w_ki, w_ki, w_wi, zeros(LANES - IDX_HEADS), w_f, zeros(LANES - B_HEADS)], axis=1).astype(BF16)
    w_fox = w_in[:, o[5]:o[8]].astype(BF16)

    big_a = _matmul(hb, w_qa.astype(BF16), BF16, 1024, 1024, head_major=True, name="proj_qa")
    big_i = _matmul(hb, w_qi.astype(BF16), BF16, 1024, 1024, head_major=True, name="proj_qidx")
    big_b = _matmul(hb, w_fox, BF16, 1024, 1024, head_major=True, name="proj_fox")
    small = _matmul(hb, w_small, F32, 1024, w_small.shape[1], name="proj_small")

    bf_pad = jnp.zeros((1, LANES), F32).at[0, :B_HEADS].set(b_forget)
    idx_scale = (IDX_DIM ** -0.5) * (IDX_HEADS ** -0.5)
    ckv, kk, widx, cum, cumt = _prep(small, g_latent.reshape(1, -1), bf_pad, bsz, seq, idx_scale)

    tq = min(512, seq)
    cumr = cumt[:, :B_HEADS, :].reshape(bsz, B_HEADS, seq // tq, 1, tq)
    o_b = _fox(big_b, cum, cumr, bsz, seq, tq, 0, wb // LANES, 2 * wb // LANES)

    wuk_t = (jnp.transpose(w_uk, (1, 2, 0)) * (A_HEAD_DIM ** -0.5)).astype(BF16)
    wuv_h = jnp.transpose(w_uv, (1, 0, 2)).astype(BF16)
    o_a = _dsa(big_i, big_a, kk, widx, ckv, wuk_t, wuv_h, bsz, seq)

    merged = _merge(hb, o_a, o_b, w_gate.astype(BF16), b_gate.reshape(1, 2 * d),
                    w_branch_a.astype(BF16), w_branch_b.astype(BF16), 1024, 256)
    hpre = _matmul_residual(merged, w_out.astype(BF16), h, alpha, 1024, 1024)
    h1, h1b, h1t = _ln1(hpre, ln1_g.reshape(1, d), ln1_b.reshape(1, d), 256)

    heads, qd = peer_wq.shape[1], peer_wq.shape[2]
    keys = peer_subkeys.shape[2]
    qp = _matmul(h1b, peer_wq.reshape(d, heads * qd).astype(BF16), BF16, 1024, 1024, head_major=True,
                 name="peer_q")
    tn = min(512, n)
    e_blk = 512
    t1r, e1r, s2, e2 = _route(qp, peer_subkeys.astype(BF16), LANES, e_blk // keys)
    v_t = _transpose_cast(peer_v, BF16, 1024, 1024)
    yt = _experts(h1t, peer_u.astype(BF16), v_t, t1r, e1r, s2, e2, tn, e_blk)

    ple = _ple(h1b, p_i.astype(BF16), w_ple_gate.astype(BF16), b_ple_gate.reshape(1, d), w_ple.astype(BF16),
               1024, 1024)
    return _ln2(h1, yt, ple, ln2_g.reshape(1, d), ln2_b.reshape(1, d), alpha, 256)


def kernel(x, p, w_in, b_forget, g_latent, w_uk, w_uv, w_branch_a, w_branch_b, w_gate, b_gate, w_out, ln1_g,
           ln1_b, peer_wq, peer_subkeys, peer_u, peer_v, w_ple, w_ple_gate, b_ple_gate, ln2_g, ln2_b):
    bsz, seq, d = x.shape
    depth = w_in.shape[0]
    alpha = (2.0 * depth) ** 0.25
    h = x.reshape(bsz * seq, d)
    for i in range(depth):
        h = _layer(h, h.astype(BF16), p[i].reshape(bsz * seq, -1), w_in[i], b_forget[i], g_latent[i], w_uk[i],
                   w_uv[i], w_branch_a[i], w_branch_b[i], w_gate[i], b_gate[i], w_out[i], ln1_g[i], ln1_b[i],
                   peer_wq[i], peer_subkeys[i], peer_u[i], peer_v[i], w_ple[i], w_ple_gate[i], b_ple_gate[i],
                   ln2_g[i], ln2_b[i], bsz, seq, alpha)
    return h.reshape(bsz, seq, d)
```

```python
import functools

import jax
import jax.numpy as jnp
from jax import lax
from jax.experimental import pallas as pl
from jax.experimental.pallas import tpu as pltpu

F32 = jnp.float32
BF16 = jnp.bfloat16
I32 = jnp.int32

LANES = 128
SUBLANES = 8
CHUNK = 64
A_HEADS = 16
A_HEAD_DIM = 128
A_LATENT = 256
IDX_HEADS = 32
IDX_DIM = 64
TOPK_MAX = 256
B_HEADS = 16
B_HEAD_DIM = 128
PEER_TOPK = 16
LN_EPS = 1e-5
INT_MIN = -2 ** 31
MIB = 1024 * 1024

VMEM_MATMUL_MIB = 56
VMEM_STREAM_MIB = 48
VMEM_EXPERTS_MIB = 60
MM_TILE = 1024
MERGE_COLS = 256
LN_ROWS = 256
FOX_BLOCK = 512
PEER_TOKENS = 512
PEER_EXPERTS = 512
PEER_OUT_ROWS = 256
PEER_GATE_ROWS = 32


def _params(n_grid, vmem_mib):
    return pltpu.CompilerParams(dimension_semantics=("arbitrary",) * n_grid,
                                vmem_limit_bytes=vmem_mib * MIB)


def _dot(a, b):
    return jnp.dot(a, b, preferred_element_type=F32)


def _dot_nt(a, b):
    return lax.dot_general(a, b, (((1,), (1,)), ((), ())), preferred_element_type=F32)


def _mm_kernel(a_ref, b_ref, o_ref, *, groups):
    acc = _dot(a_ref[...], b_ref[...])
    if groups is None:
        o_ref[...] = acc.astype(o_ref.dtype)
    else:
        for g in range(groups):
            o_ref[g] = acc[:, g * LANES:(g + 1) * LANES].astype(o_ref.dtype)


def _matmul(a, b, out_dtype, bm, bn, head_major=False, name="mm"):
    m, k = a.shape
    n = b.shape[1]
    bm, bn = min(bm, m), min(bn, n)
    assert m % bm == 0 and n % bn == 0
    if head_major:
        out_shape = jax.ShapeDtypeStruct((n // LANES, m, LANES), out_dtype)
        out_spec = pl.BlockSpec((bn // LANES, bm, LANES), lambda j, i: (j, i, 0))
        groups = bn // LANES
    else:
        out_shape = jax.ShapeDtypeStruct((m, n), out_dtype)
        out_spec = pl.BlockSpec((bm, bn), lambda j, i: (i, j))
        groups = None
    return pl.pallas_call(
        functools.partial(_mm_kernel, groups=groups),
        grid=(n // bn, m // bm),
        in_specs=[pl.BlockSpec((bm, k), lambda j, i: (i, 0)),
                  pl.BlockSpec((k, bn), lambda j, i: (0, j))],
        out_specs=out_spec, out_shape=out_shape,
        compiler_params=_params(2, VMEM_MATMUL_MIB), name=name)(a, b)


def _mm_res_kernel(a_ref, b_ref, x_ref, o_ref, *, alpha):
    o_ref[...] = alpha * x_ref[...] + _dot(a_ref[...], b_ref[...])


def _matmul_residual(a, b, x, alpha, bm, bn):
    m, k = a.shape
    n = b.shape[1]
    bm, bn = min(bm, m), min(bn, n)
    return pl.pallas_call(
        functools.partial(_mm_res_kernel, alpha=alpha),
        grid=(n // bn, m // bm),
        in_specs=[pl.BlockSpec((bm, k), lambda j, i: (i, 0)),
                  pl.BlockSpec((k, bn), lambda j, i: (0, j)),
                  pl.BlockSpec((bm, bn), lambda j, i: (i, j))],
        out_specs=pl.BlockSpec((bm, bn), lambda j, i: (i, j)),
        out_shape=jax.ShapeDtypeStruct((m, n), F32),
        compiler_params=_params(2, VMEM_MATMUL_MIB), name="outproj")(a, b, x)


def _merge_kernel(x_ref, oa_ref, ob_ref, wga_ref, wgb_ref, bga_ref, bgb_ref, wba_ref, wbb_ref, o_ref):
    x = x_ref[...]
    ga = jax.nn.sigmoid(_dot(x, wga_ref[...]) + bga_ref[...])
    gb = jax.nn.sigmoid(_dot(x, wgb_ref[...]) + bgb_ref[...])
    ya = _dot(oa_ref[...], wba_ref[...])
    yb = _dot(ob_ref[...], wbb_ref[...])
    o_ref[...] = (ga * ya + gb * yb).astype(o_ref.dtype)


def _merge(xb, oa, ob, wg, bg, wba, wbb, bm, bn):
    m, d = xb.shape
    ka, kb = oa.shape[1], ob.shape[1]
    n = wba.shape[1]
    bm, bn = min(bm, m), min(bn, n)
    nj = n // bn
    row = lambda i, j: (i, 0)
    col = lambda i, j: (0, j)
    colb = lambda i, j: (0, nj + j)
    return pl.pallas_call(
        _merge_kernel,
        grid=(m // bm, nj),
        in_specs=[pl.BlockSpec((bm, d), row), pl.BlockSpec((bm, ka), row), pl.BlockSpec((bm, kb), row),
                  pl.BlockSpec((d, bn), col), pl.BlockSpec((d, bn), colb),
                  pl.BlockSpec((1, bn), col), pl.BlockSpec((1, bn), colb),
                  pl.BlockSpec((ka, bn), col), pl.BlockSpec((kb, bn), col)],
        out_specs=pl.BlockSpec((bm, bn), lambda i, j: (i, j)),
        out_shape=jax.ShapeDtypeStruct((m, n), BF16),
        compiler_params=_params(2, VMEM_MATMUL_MIB), name="merge")(xb, oa, ob, wg, wg, bg, bg, wba, wbb)


def _ple_kernel(h_ref, p_ref, wg_ref, bg_ref, wp_ref, o_ref):
    g = jax.nn.sigmoid(_dot(h_ref[...], wg_ref[...]) + bg_ref[...])
    o_ref[...] = (g * _dot(p_ref[...], wp_ref[...])).astype(o_ref.dtype)


def _ple(hb, pb, wg, bg, wp, bm, bn):
    m, d = hb.shape
    dp = pb.shape[1]
    n = wg.shape[1]
    bm, bn = min(bm, m), min(bn, n)
    return pl.pallas_call(
        _ple_kernel,
        grid=(n // bn, m // bm),
        in_specs=[pl.BlockSpec((bm, d), lambda j, i: (i, 0)), pl.BlockSpec((bm, dp), lambda j, i: (i, 0)),
                  pl.BlockSpec((d, bn), lambda j, i: (0, j)), pl.BlockSpec((1, bn), lambda j, i: (0, j)),
                  pl.BlockSpec((dp, bn), lambda j, i: (0, j))],
        out_specs=pl.BlockSpec((bm, bn), lambda j, i: (i, j)),
        out_shape=jax.ShapeDtypeStruct((m, n), BF16),
        compiler_params=_params(2, VMEM_MATMUL_MIB), name="ple")(hb, pb, wg, bg, wp)


def _transpose_cast_kernel(x_ref, o_ref):
    o_ref[...] = x_ref[...].T.astype(o_ref.dtype)


def _transpose_cast(x, dtype, br, bc):
    r, c = x.shape
    br, bc = min(br, r), min(bc, c)
    return pl.pallas_call(
        _transpose_cast_kernel,
        grid=(r // br, c // bc),
        in_specs=[pl.BlockSpec((br, bc), lambda i, j: (i, j))],
        out_specs=pl.BlockSpec((bc, br), lambda i, j: (j, i)),
        out_shape=jax.ShapeDtypeStruct((c, r), dtype),
        compiler_params=_params(2, VMEM_STREAM_MIB), name="transpose_cast")(x)


def _layer_norm_rows(v, g, b):
    mu = jnp.mean(v, axis=-1, keepdims=True)
    d = v - mu
    var = jnp.mean(d * d, axis=-1, keepdims=True)
    return d * lax.rsqrt(var + LN_EPS) * g + b


def _ln1_kernel(hp_ref, g_ref, b_ref, h_ref, hb_ref, ht_ref):
    y = _layer_norm_rows(hp_ref[...], g_ref[...], b_ref[...])
    h_ref[...] = y
    hb_ref[...] = y.astype(BF16)
    ht_ref[...] = y.T.astype(BF16)


def _ln1(hp, g, b, bm):
    m, d = hp.shape
    bm = min(bm, m)
    return pl.pallas_call(
        _ln1_kernel,
        grid=(m // bm,),
        in_specs=[pl.BlockSpec((bm, d), lambda i: (i, 0)),
                  pl.BlockSpec((1, d), lambda i: (0, 0)), pl.BlockSpec((1, d), lambda i: (0, 0))],
        out_specs=[pl.BlockSpec((bm, d), lambda i: (i, 0)), pl.BlockSpec((bm, d), lambda i: (i, 0)),
                   pl.BlockSpec((d, bm), lambda i: (0, i))],
        out_shape=[jax.ShapeDtypeStruct((m, d), F32), jax.ShapeDtypeStruct((m, d), BF16),
                   jax.ShapeDtypeStruct((d, m), BF16)],
        compiler_params=_params(1, VMEM_STREAM_MIB), name="ln1")(hp, g, b)


def _ln2_kernel(h_ref, yt_ref, ple_ref, g_ref, b_ref, o_ref, *, alpha):
    v = alpha * h_ref[...] + yt_ref[...].astype(F32).T + ple_ref[...].astype(F32)
    o_ref[...] = _layer_norm_rows(v, g_ref[...], b_ref[...])


def _ln2(h, yt, ple, g, b, alpha, bm):
    m, d = h.shape
    bm = min(bm, m)
    return pl.pallas_call(
        functools.partial(_ln2_kernel, alpha=alpha),
        grid=(m // bm,),
        in_specs=[pl.BlockSpec((bm, d), lambda i: (i, 0)), pl.BlockSpec((d, bm), lambda i: (0, i)),
                  pl.BlockSpec((bm, d), lambda i: (i, 0)),
                  pl.BlockSpec((1, d), lambda i: (0, 0)), pl.BlockSpec((1, d), lambda i: (0, 0))],
        out_specs=pl.BlockSpec((bm, d), lambda i: (i, 0)),
        out_shape=jax.ShapeDtypeStruct((m, d), F32),
        compiler_params=_params(1, VMEM_STREAM_MIB), name="ln2")(h, yt, ple, g, b)


def _prep_kernel(sm_ref, gl_ref, bf_ref, ckv_ref, kk_ref, w_ref, cum_ref, cumt_ref, *, idx_scale, blk):
    s = sm_ref.shape[0]
    c = sm_ref[:, 0:A_LATENT]
    ms = jnp.mean(c * c, axis=-1, keepdims=True)
    ckv_ref[...] = (c * lax.rsqrt(ms + LN_EPS) * gl_ref[...]).astype(BF16)
    o = A_LATENT
    kk_ref[...] = sm_ref[:, o:o + LANES].astype(BF16)
    w_ref[...] = sm_ref[:, o + LANES:o + 2 * LANES] * idx_scale
    f = sm_ref[:, o + 2 * LANES:o + 3 * LANES] + bf_ref[...]
    ls = jnp.minimum(f, 0.0) - jnp.log1p(jnp.exp(-jnp.abs(f)))
    r = lax.broadcasted_iota(I32, (blk, blk), 0)
    cc = lax.broadcasted_iota(I32, (blk, blk), 1)
    tri = jnp.where(r >= cc, 1.0, 0.0).astype(BF16)
    carry = jnp.zeros((1, LANES), F32)
    for kb in range(s // blk):
        xk = ls[kb * blk:(kb + 1) * blk]
        hi = xk.astype(BF16)
        r1 = xk - hi.astype(F32)
        mid = r1.astype(BF16)
        lo = (r1 - mid.astype(F32)).astype(BF16)
        ck = _dot(tri, hi) + _dot(tri, mid) + _dot(tri, lo) + carry
        cum_ref[kb * blk:(kb + 1) * blk, :] = ck
        carry = ck[blk - 1:blk, :]
    cumt_ref[...] = cum_ref[...].T


def _prep(small, g_latent, b_forget_pad, bsz, seq, idx_scale):
    n, w = small.shape
    blk = min(256, seq)
    return pl.pallas_call(
        functools.partial(_prep_kernel, idx_scale=idx_scale, blk=blk),
        grid=(bsz,),
        in_specs=[pl.BlockSpec((seq, w), lambda b: (b, 0)),
                  pl.BlockSpec((1, A_LATENT), lambda b: (0, 0)), pl.BlockSpec((1, LANES), lambda b: (0, 0))],
        out_specs=[pl.BlockSpec((None, seq, A_LATENT), lambda b: (b, 0, 0)),
                   pl.BlockSpec((None, seq, LANES), lambda b: (b, 0, 0)),
                   pl.BlockSpec((seq, LANES), lambda b: (b, 0)),
                   pl.BlockSpec((seq, LANES), lambda b: (b, 0)),
                   pl.BlockSpec((None, LANES, seq), lambda b: (b, 0, 0))],
        out_shape=[jax.ShapeDtypeStruct((bsz, seq, A_LATENT), BF16),
                   jax.ShapeDtypeStruct((bsz, seq, LANES), BF16),
                   jax.ShapeDtypeStruct((n, LANES), F32),
                   jax.ShapeDtypeStruct((n, LANES), F32),
                   jax.ShapeDtypeStruct((bsz, LANES, seq), F32)],
        compiler_params=_params(1, VMEM_STREAM_MIB), name="prep")(small, g_latent, b_forget_pad)


FOX_HEADS_PER_STEP = 2


def _fox_kernel(q_ref, k_ref, v_ref, cumc_ref, cumr_ref, o_ref, *, tq, scale):
    hp = pl.program_id(1)
    i = pl.program_id(2)
    nh = FOX_HEADS_PER_STEP
    lane = lax.broadcasted_iota(I32, (tq, LANES), 1)
    qs = [(q_ref[e].astype(F32) * scale).astype(BF16) for e in range(nh)]
    cqs = [jnp.sum(jnp.where(lane == hp * nh + e, cumc_ref[...], 0.0), axis=1, keepdims=True)
           for e in range(nh)]

    def block(j, carry, diagonal):
        off = pl.multiple_of(j * tq, tq)
        out = []
        for e in range(nh):
            m, l, acc = carry[e]
            k = k_ref[e, pl.ds(off, tq), :]
            v = v_ref[e, pl.ds(off, tq), :]
            s = _dot_nt(qs[e], k) + (cqs[e] - cumr_ref[e, j])
            if diagonal:
                row = lax.broadcasted_iota(I32, (tq, tq), 0)
                col = lax.broadcasted_iota(I32, (tq, tq), 1)
                s = jnp.where(col <= row, s, -jnp.inf)
            m_new = jnp.maximum(m, jnp.max(s, axis=1, keepdims=True))
            p = jnp.exp(s - m_new)
            a = jnp.exp(m - m_new)
            l = a * l + jnp.sum(p, axis=1, keepdims=True)
            acc = a * acc + _dot(p.astype(BF16), v)
            out.append((m_new, l, acc))
        return tuple(out)

    init = (jnp.full((tq, 1), -jnp.inf, F32), jnp.zeros((tq, 1), F32), jnp.zeros((tq, B_HEAD_DIM), F32))
    carry = lax.fori_loop(0, i, functools.partial(block, diagonal=False), (init,) * nh)
    final = block(i, carry, True)
    for e in range(nh):
        _, l, acc = final[e]
        o_ref[:, e * B_HEAD_DIM:(e + 1) * B_HEAD_DIM] = (acc / l).astype(o_ref.dtype)


def _fox(big, cum, cumr, bsz, seq, tq, g_q, g_k, g_v):
    n = big.shape[1]
    tq = min(tq, seq)
    nq = seq // tq
    nh = FOX_HEADS_PER_STEP
    assert g_q % nh == 0 and g_k % nh == 0 and g_v % nh == 0 and B_HEADS % nh == 0
    return pl.pallas_call(
        functools.partial(_fox_kernel, tq=tq, scale=B_HEAD_DIM ** -0.5),
        grid=(bsz, B_HEADS // nh, nq),
        in_specs=[pl.BlockSpec((nh, tq, LANES), lambda b, h, i: (g_q // nh + h, b * nq + i, 0)),
                  pl.BlockSpec((nh, seq, LANES), lambda b, h, i: (g_k // nh + h, b, 0)),
                  pl.BlockSpec((nh, seq, LANES), lambda b, h, i: (g_v // nh + h, b, 0)),
                  pl.BlockSpec((tq, LANES), lambda b, h, i: (b * nq + i, 0)),
                  pl.BlockSpec((None, nh, nq, 1, tq), lambda b, h, i: (b, h, 0, 0, 0))],
        out_specs=pl.BlockSpec((tq, nh * B_HEAD_DIM), lambda b, h, i: (b * nq + i, h)),
        out_shape=jax.ShapeDtypeStruct((n, B_HEADS * B_HEAD_DIM), BF16),
        compiler_params=_params(3, VMEM_STREAM_MIB), name="fox")(big, big, big, cum, cumr)


def _dsa_kernel(qi_ref, qa_ref, kk_ref, w_ref, ckv_ref, wuk_ref, wuv_ref, o_ref,
                qd_ref, wt_ref, key_ref, bias_ref, jstar_ref, qlat_ref, s_ref, mp_ref, lp_ref, p_ref, acc_ref,
                *, tq, seq, kc, k_sel):
    i = pl.program_id(1)
    t0 = i * tq
    nk = (t0 + tq + kc - 1) // kc
    npair = IDX_HEADS // 2
    rows_all = A_HEADS * tq
    slopes = [2.0 ** (-8.0 * (h + 1) / A_HEADS) for h in range(A_HEADS)]

    lane = lax.broadcasted_iota(I32, (tq, LANES), 1)
    for j in range(npair):
        a = qi_ref[j].astype(F32)
        qd_ref[j, 0:tq, :] = jnp.where(lane < IDX_DIM, a, 0.0).astype(BF16)
        qd_ref[j, tq:2 * tq, :] = jnp.where(lane >= IDX_DIM, a, 0.0).astype(BF16)
    wt_ref[...] = w_ref[...].T

    qpos = t0 + lax.broadcasted_iota(I32, (1, tq), 1)
    chunk_end = (qpos // CHUNK + 1) * CHUNK

    def idx_chunk(c, carry):
        kkc = kk_ref[c]
        acc = jnp.zeros((kc, tq), F32)
        for j in range(npair):
            out = _dot_nt(kkc, qd_ref[j])
            acc = acc + jnp.maximum(out[:, :tq], 0.0) * wt_ref[2 * j:2 * j + 1, :]
            acc = acc + jnp.maximum(out[:, tq:], 0.0) * wt_ref[2 * j + 1:2 * j + 2, :]
        kidx = c * kc + lax.broadcasted_iota(I32, (kc, tq), 0)
        bits = pltpu.bitcast(acc, I32)
        key = bits ^ ((bits >> 31) & 0x7FFFFFFF)
        key_ref[c] = jnp.where(kidx < chunk_end, key, INT_MIN)
        return carry

    lax.fori_loop(0, nk, idx_chunk, 0)

    sub = lax.broadcasted_iota(I32, (SUBLANES, tq), 0)

    def count(pred):
        nacc = 4

        def body(c, accs):
            kch = key_ref[c]
            accs = list(accs)
            for g in range(kc // SUBLANES):
                kg = kch[g * SUBLANES:(g + 1) * SUBLANES, :]
                accs[g % nacc] = accs[g % nacc] + jnp.where(pred(kg, c * kc + g * SUBLANES + sub), 1.0, 0.0)
            return tuple(accs)
        accs = lax.fori_loop(0, nk, body, (jnp.zeros((SUBLANES, tq), F32),) * nacc)
        acc = (accs[0] + accs[1]) + (accs[2] + accs[3])
        return jnp.broadcast_to(jnp.sum(acc, axis=0, keepdims=True), (SUBLANES, tq))

    def bis(it, tu):
        cand_u = tu | lax.shift_left(jnp.int32(1), 31 - it)
        cand_s = cand_u ^ INT_MIN
        cnt = count(lambda k, _: k >= cand_s)
        return jnp.where(cnt >= k_sel, cand_u, tu)

    tu = lax.fori_loop(0, 32, bis, jnp.zeros((SUBLANES, tq), I32))
    thr = tu ^ INT_MIN
    cnt_ge = count(lambda k, _: k >= thr)
    tie = jnp.where((cnt_ge > k_sel) & (thr != INT_MIN), 1, 0)
    jstar_ref[...] = jnp.full((SUBLANES, tq), seq, I32)

    @pl.when(jnp.max(tie) > 0)
    def _():
        need = k_sel - count(lambda k, _: k > thr)
        nbits = seq.bit_length()

        def bis2(it, jj):
            cand = jj | lax.shift_left(jnp.int32(1), nbits - 1 - it)
            f = count(lambda k, kidx: (k == thr) & (kidx < cand))
            return jnp.where((cand <= seq) & (f <= need), cand, jj)

        jstar_ref[...] = lax.fori_loop(0, nbits, bis2, jnp.zeros((SUBLANES, tq), I32))

    thr1 = thr[0:1, :]
    jst1 = jstar_ref[0:1, :]

    def bias_chunk(c, carry):
        kch = key_ref[c]
        kidx = c * kc + lax.broadcasted_iota(I32, (kc, tq), 0)
        sel = ((kch > thr1) | ((kch == thr1) & (kidx < jst1))) & (kidx < chunk_end)
        bias_ref[c] = jnp.where(sel, 0.0, -jnp.inf).T
        return carry

    lax.fori_loop(0, nk, bias_chunk, 0)

    for h in range(A_HEADS):
        qlat_ref[h * tq:(h + 1) * tq, :] = _dot(qa_ref[h], wuk_ref[h]).astype(BF16)
    mp_ref[...] = jnp.full((rows_all, LANES), -jnp.inf, F32)
    rowq = t0 + lax.broadcasted_iota(I32, (tq, kc), 0)
    colk = lax.broadcasted_iota(I32, (tq, kc), 1)

    def pass1(c, carry):
        s = _dot_nt(qlat_ref[...], ckv_ref[c])
        b = bias_ref[c]
        dist = jnp.abs(rowq - (colk + c * kc)).astype(F32)
        for h in range(A_HEADS):
            r = slice(h * tq, (h + 1) * tq)
            sh = s[r] - slopes[h] * dist + b
            s_ref[c, r, :] = sh
            mp = mp_ref[r, :]
            for u in range(kc // LANES):
                mp = jnp.maximum(mp, sh[:, u * LANES:(u + 1) * LANES])
            mp_ref[r, :] = mp
        return carry

    lax.fori_loop(0, nk, pass1, 0)
    m = jnp.max(mp_ref[...], axis=1, keepdims=True)
    mp_ref[...] = jnp.broadcast_to(m, (rows_all, LANES))
    lp_ref[...] = jnp.zeros((rows_all, LANES), F32)
    acc_ref[...] = jnp.zeros((rows_all, A_LATENT), F32)

    def pass2(c, carry):
        for h in range(A_HEADS):
            r = slice(h * tq, (h + 1) * tq)
            mrep = mp_ref[r, :]
            lp = lp_ref[r, :]
            for u in range(kc // LANES):
                p = jnp.exp(s_ref[c, r, u * LANES:(u + 1) * LANES] - mrep)
                lp = lp + p
                p_ref[r, u * LANES:(u + 1) * LANES] = p.astype(BF16)
            lp_ref[r, :] = lp
        acc_ref[...] += _dot(p_ref[...], ckv_ref[c])
        return carry

    lax.fori_loop(0, nk, pass2, 0)
    l = jnp.sum(lp_ref[...], axis=1, keepdims=True)
    olat = (acc_ref[...] / l).astype(BF16)
    for h in range(A_HEADS):
        o_ref[:, h * A_HEAD_DIM:(h + 1) * A_HEAD_DIM] = _dot(
            olat[h * tq:(h + 1) * tq], wuv_ref[h]).astype(o_ref.dtype)


def _dsa(big_i, big_a, kk, widx, ckv, wuk_t, wuv_h, bsz, seq):
    n = big_a.shape[1]
    tq = LANES
    assert seq % tq == 0
    kc = min(256, seq)
    nq, nkc = seq // tq, seq // kc
    k_sel = min(TOPK_MAX, seq // 4)
    npair = IDX_HEADS // 2
    rows_all = A_HEADS * tq
    kk = kk.reshape(bsz, nkc, kc, LANES)
    ckv = ckv.reshape(bsz, nkc, kc, A_LATENT)
    return pl.pallas_call(
        functools.partial(_dsa_kernel, tq=tq, seq=seq, kc=kc, k_sel=k_sel),
        grid=(bsz, nq),
        in_specs=[pl.BlockSpec((npair, tq, LANES), lambda b, i: (0, b * nq + i, 0)),
                  pl.BlockSpec((A_HEADS, tq, LANES), lambda b, i: (0, b * nq + i, 0)),
                  pl.BlockSpec((None, nkc, kc, LANES), lambda b, i: (b, 0, 0, 0)),
                  pl.BlockSpec((tq, LANES), lambda b, i: (b * nq + i, 0)),
                  pl.BlockSpec((None, nkc, kc, A_LATENT), lambda b, i: (b, 0, 0, 0)),
                  pl.BlockSpec((A_HEADS, A_HEAD_DIM, A_LATENT), lambda b, i: (0, 0, 0)),
                  pl.BlockSpec((A_HEADS, A_LATENT, A_HEAD_DIM), lambda b, i: (0, 0, 0))],
        out_specs=pl.BlockSpec((tq, A_HEADS * A_HEAD_DIM), lambda b, i: (b * nq + i, 0)),
        out_shape=jax.ShapeDtypeStruct((n, A_HEADS * A_HEAD_DIM), BF16),
        scratch_shapes=[pltpu.VMEM((npair, 2 * tq, LANES), BF16),
                        pltpu.VMEM((LANES, tq), F32),
                        pltpu.VMEM((nkc, kc, tq), I32),
                        pltpu.VMEM((nkc, tq, kc), F32),
                        pltpu.VMEM((SUBLANES, tq), I32),
                        pltpu.VMEM((rows_all, A_LATENT), BF16),
                        pltpu.VMEM((nkc, rows_all, kc), F32),
                        pltpu.VMEM((rows_all, LANES), F32),
                        pltpu.VMEM((rows_all, LANES), F32),
                        pltpu.VMEM((rows_all, kc), BF16),
                        pltpu.VMEM((rows_all, A_LATENT), F32)],
        compiler_params=_params(2, VMEM_MATMUL_MIB), name="dsa")(big_i, big_a, kk, widx, ckv, wuk_t, wuv_h)


def _topk_desc(x, k):
    out = []
    cur = x
    for _ in range(k):
        m = jnp.max(cur, axis=0, keepdims=True)
        out.append(m)
        cur = jnp.where(cur == m, -jnp.inf, cur)
    return out


def _route_kernel(qp_ref, sk_ref, t1_ref, e1_ref, s2_ref, e2_ref, v2_ref, cand_ref, *, heads, rows):
    keys = s2_ref.shape[1]
    k1 = PEER_TOPK + 1
    for h in range(heads):
        s1 = _dot_nt(sk_ref[h, 0], qp_ref[2 * h])
        s2 = _dot_nt(sk_ref[h, 1], qp_ref[2 * h + 1])
        v1 = _topk_desc(s1, k1)
        v2 = _topk_desc(s2, k1)
        for a in range(k1):
            v2_ref[a:a + 1, :] = v2[a]
        off = 0
        for a in range(k1):
            nb = k1 // (a + 1)
            cand_ref[off:off + nb, :] = v1[a] + v2_ref[0:nb, :]
            off += nb
        cand_ref[off:, :] = jnp.full((cand_ref.shape[0] - off, cand_ref.shape[1]), -jnp.inf, F32)
        cand = cand_ref[...]
        tops = _topk_desc(cand, k1)
        t_k = tops[PEER_TOPK - 1]
        t_mid = 0.5 * (t_k + tops[PEER_TOPK])
        mx = v1[0] + v2[0]
        z = jnp.sum(jnp.where(cand >= t_k, jnp.exp(cand - mx), 0.0), axis=0, keepdims=True)
        e1 = jnp.exp(s1 - v1[0]) / z
        t1 = t_mid - s1
        for blk in range(keys // rows):
            t1_ref[blk, h * rows:(h + 1) * rows, :] = t1[blk * rows:(blk + 1) * rows, :]
            e1_ref[blk, h * rows:(h + 1) * rows, :] = e1[blk * rows:(blk + 1) * rows, :]
        s2_ref[h] = s2
        e2_ref[h] = jnp.exp(s2 - v2[0])


def _route(qp, sk, tn, rows):
    g, n, _ = qp.shape
    heads = g // 2
    keys = sk.shape[2]
    tn = min(tn, n)
    nblk = keys // rows
    k1 = PEER_TOPK + 1
    ncand = sum(k1 // (a + 1) for a in range(k1))
    ncand = -(-ncand // SUBLANES) * SUBLANES
    hk = pl.BlockSpec((heads, keys, tn), lambda i: (0, 0, i))
    rk = pl.BlockSpec((nblk, heads * rows, tn), lambda i: (0, 0, i))
    sd = jax.ShapeDtypeStruct((heads, keys, n), F32)
    rd = jax.ShapeDtypeStruct((nblk, heads * rows, n), F32)
    return pl.pallas_call(
        functools.partial(_route_kernel, heads=heads, rows=rows),
        grid=(n // tn,),
        in_specs=[pl.BlockSpec((g, tn, LANES), lambda i: (0, i, 0)),
                  pl.BlockSpec(sk.shape, lambda i: (0, 0, 0, 0))],
        out_specs=[rk, rk, hk, hk],
        out_shape=[rd, rd, sd, sd],
        scratch_shapes=[pltpu.VMEM((-(-k1 // SUBLANES) * SUBLANES, tn), F32), pltpu.VMEM((ncand, tn), F32)],
        compiler_params=_params(1, VMEM_STREAM_MIB), name="route")(qp, sk)


def _expert_kernel(ht_ref, u_ref, vt_ref, t1_ref, e1_ref, s2_ref, e2_ref, y_ref,
                   act_ref, a_ref, yacc_ref, *, heads, rows, ne, dchunk, sb):
    t = pl.program_id(0)
    keys = s2_ref.shape[1]
    d, tn = yacc_ref.shape
    e_blk = act_ref.shape[1]
    act_new_ref, act_use_ref, a_new_ref, a_use_ref = act_ref.at[0], act_ref.at[1], a_ref.at[0], a_ref.at[1]

    @pl.when(t == 0)
    def _():
        act_new_ref[...] = jnp.zeros(act_new_ref.shape, F32)
        a_new_ref[...] = jnp.zeros(a_new_ref.shape, BF16)
        yacc_ref[...] = jnp.zeros(yacc_ref.shape, F32)

    a_use_ref[...] = a_new_ref[...]
    act_use_ref[...] = act_new_ref[...]

    @pl.when((t - 2) % ne == 0)
    def _():
        yacc_ref[...] = jnp.zeros(yacc_ref.shape, F32)

    def second_matmul(c, zero):
        rs = slice(c * dchunk, (c + 1) * dchunk)
        yacc_ref[rs, :] = yacc_ref[rs, :] + _dot(vt_ref[rs, :], a_use_ref[...])
        c0 = c * dchunk
        yacc_ref[c0:c0 + SUBLANES, 0:LANES] = yacc_ref[c0:c0 + SUBLANES, 0:LANES] + zero

    def gate(piece):
        lo = piece * sb
        r, q = lo // keys, lo % keys
        a = act_use_ref[lo:lo + sb, :]
        w = jnp.zeros_like(a)
        for h in range(heads):
            sel = s2_ref[h, q:q + sb, :] >= t1_ref[h * rows + r:h * rows + r + 1, :]
            g = e1_ref[h * rows + r:h * rows + r + 1, :] * e2_ref[h, q:q + sb, :]
            w = w + jnp.where(sel, g, 0.0)
        gelu = 0.5 * a * (1.0 + lax.erf(a * (2.0 ** -0.5)))
        out = w * gelu
        a_new_ref[lo:lo + sb, :] = out.astype(BF16)
        tile = out[0:SUBLANES, 0:LANES]
        for i in range(sb // SUBLANES):
            for j in range(tn // LANES):
                if i or j:
                    tile = tile + out[i * SUBLANES:(i + 1) * SUBLANES, j * LANES:(j + 1) * LANES]
        return tile

    def first_matmul(half):
        c0 = half * (tn // 2)
        act_new_ref[:, c0:c0 + tn // 2] = _dot(u_ref[...], ht_ref[:, c0:c0 + tn // 2])

    def gates_then_zero(pieces):
        tile = gate(pieces[0])
        for piece in pieces[1:]:
            tile = tile + gate(piece)
        bits = pltpu.bitcast(tile, jnp.uint32)
        return pltpu.bitcast((bits >> 16) >> 16, F32)

    nd, npieces = d // dchunk, e_blk // sb
    per = npieces // nd
    for c in range(nd):
        if c == 0:
            first_matmul(0)
        if c == nd // 2:
            first_matmul(1)
        second_matmul(c, gates_then_zero(list(range(c * per, (c + 1) * per))))

    @pl.when((t - 2) % ne == ne - 1)
    def _():
        y_ref[...] = yacc_ref[...].astype(y_ref.dtype)


def _experts(ht, u, vt, t1r, e1r, s2, e2, tn, e_blk):
    d, n = ht.shape
    heads, keys, _ = s2.shape
    rows = e_blk // keys
    tn = min(tn, n)
    ne = u.shape[0] // e_blk
    last = (n // tn) * ne - 1
    st1 = lambda t: jnp.minimum(t, last)
    st2 = lambda t: jnp.clip(t - 1, 0, last)
    st3 = lambda t: jnp.clip(t - 2, 0, last)
    return pl.pallas_call(
        functools.partial(_expert_kernel, heads=heads, rows=rows, ne=ne,
                          dchunk=min(PEER_OUT_ROWS, d), sb=PEER_GATE_ROWS),
        grid=(last + 3,),
        in_specs=[pl.BlockSpec((d, tn), lambda t: (0, st1(t) // ne)),
                  pl.BlockSpec((e_blk, d), lambda t: (st1(t) % ne, 0)),
                  pl.BlockSpec((d, e_blk), lambda t: (0, st3(t) % ne)),
                  pl.BlockSpec((None, heads * rows, tn), lambda t: (st2(t) % ne, 0, st2(t) // ne)),
                  pl.BlockSpec((None, heads * rows, tn), lambda t: (st2(t) % ne, 0, st2(t) // ne)),
                  pl.BlockSpec((heads, keys, tn), lambda t: (0, 0, st2(t) // ne)),
                  pl.BlockSpec((heads, keys, tn), lambda t: (0, 0, st2(t) // ne))],
        out_specs=pl.BlockSpec((d, tn), lambda t: (0, st3(t) // ne)),
        out_shape=jax.ShapeDtypeStruct((d, n), BF16),
        scratch_shapes=[pltpu.VMEM((2, e_blk, tn), F32), pltpu.VMEM((2, e_blk, tn), BF16),
                        pltpu.VMEM((d, tn), F32)],
        compiler_params=_params(1, VMEM_EXPERTS_MIB), name="experts")(ht, u, vt, t1r, e1r, s2, e2)


def _layer(h, hb, p_i, w_in, b_forget, g_latent, w_uk, w_uv, w_branch_a, w_branch_b, w_gate, b_gate, w_out,
           ln1_g, ln1_b, peer_wq, peer_subkeys, peer_u, peer_v, w_ple, w_ple_gate, b_ple_gate, ln2_g, ln2_b,
           bsz, seq, alpha):
    n, d = h.shape
    wa, wl, wi = A_HEADS * A_HEAD_DIM, A_LATENT, IDX_HEADS * IDX_DIM
    wb = B_HEADS * B_HEAD_DIM
    o = [0, wa, wa + wl, wa + wl + wi, wa + wl + wi + IDX_DIM, wa + wl + wi + IDX_DIM + IDX_HEADS]
    o += [o[-1] + wb, o[-1] + 2 * wb, o[-1] + 3 * wb, o[-1] + 3 * wb + B_HEADS]
    seg = [w_in[:, o[k]:o[k + 1]] for k in range(9)]
    (w_qa, w_ckv, w_qi, w_ki, w_wi, w_qb, w_kb, w_vb, w_f) = seg
    zeros = lambda c: jnp.zeros((d, c), w_in.dtype)
    w_small = jnp.concatenate(
        [w_ckv, w_ki, w_ki, w_wi, zeros(LANES - IDX_HEADS), w_f, zeros(LANES - B_HEADS)], axis=1).astype(BF16)
    w_fox = w_in[:, o[5]:o[8]].astype(BF16)

    big_a = _matmul(hb, w_qa.astype(BF16), BF16, MM_TILE, MM_TILE, head_major=True, name="proj_qa")
    big_i = _matmul(hb, w_qi.astype(BF16), BF16, MM_TILE, MM_TILE, head_major=True, name="proj_qidx")
    big_b = _matmul(hb, w_fox, BF16, MM_TILE, MM_TILE, head_major=True, name="proj_fox")
    small = _matmul(hb, w_small, F32, MM_TILE, w_small.shape[1], name="proj_small")

    bf_pad = jnp.zeros((1, LANES), F32).at[0, :B_HEADS].set(b_forget)
    idx_scale = (IDX_DIM ** -0.5) * (IDX_HEADS ** -0.5)
    ckv, kk, widx, cum, cumt = _prep(small, g_latent.reshape(1, -1), bf_pad, bsz, seq, idx_scale)

    tq = min(FOX_BLOCK, seq)
    cumr = cumt[:, :B_HEADS, :].reshape(bsz, B_HEADS, seq // tq, 1, tq)
    o_b = _fox(big_b, cum, cumr, bsz, seq, tq, 0, wb // LANES, 2 * wb // LANES)

    wuk_t = (jnp.transpose(w_uk, (1, 2, 0)) * (A_HEAD_DIM ** -0.5)).astype(BF16)
    wuv_h = jnp.transpose(w_uv, (1, 0, 2)).astype(BF16)
    o_a = _dsa(big_i, big_a, kk, widx, ckv, wuk_t, wuv_h, bsz, seq)

    merged = _merge(hb, o_a, o_b, w_gate.astype(BF16), b_gate.reshape(1, 2 * d),
                    w_branch_a.astype(BF16), w_branch_b.astype(BF16), MM_TILE, MERGE_COLS)
    hpre = _matmul_residual(merged, w_out.astype(BF16), h, alpha, MM_TILE, MM_TILE)
    h1, h1b, h1t = _ln1(hpre, ln1_g.reshape(1, d), ln1_b.reshape(1, d), LN_ROWS)

    heads, qd = peer_wq.shape[1], peer_wq.shape[2]
    keys = peer_subkeys.shape[2]
    qp = _matmul(h1b, peer_wq.reshape(d, heads * qd).astype(BF16), BF16, MM_TILE, MM_TILE, head_major=True,
                 name="peer_q")
    tn = min(PEER_TOKENS, n)
    e_blk = PEER_EXPERTS
    t1r, e1r, s2, e2 = _route(qp, peer_subkeys.astype(BF16), LANES, e_blk // keys)
    v_t = _transpose_cast(peer_v, BF16, MM_TILE, MM_TILE)
    yt = _experts(h1t, peer_u.astype(BF16), v_t, t1r, e1r, s2, e2, tn, e_blk)

    ple = _ple(h1b, p_i.astype(BF16), w_ple_gate.astype(BF16), b_ple_gate.reshape(1, d), w_ple.astype(BF16),
               MM_TILE, MM_TILE)
    return _ln2(h1, yt, ple, ln2_g.reshape(1, d), ln2_b.reshape(1, d), alpha, LN_ROWS)


def kernel(x, p, w_in, b_forget, g_latent, w_uk, w_uv, w_branch_a, w_branch_b, w_gate, b_gate, w_out, ln1_g,
           ln1_b, peer_wq, peer_subkeys, peer_u, peer_v, w_ple, w_ple_gate, b_ple_gate, ln2_g, ln2_b):
    bsz, seq, d = x.shape
    depth = w_in.shape[0]
    alpha = (2.0 * depth) ** 0.25
    h = x.reshape(bsz * seq, d)
    for i in range(depth):
        h = _layer(h, h.astype(BF16), p[i].reshape(bsz * seq, -1), w_in[i], b_forget[i], g_latent[i], w_uk[i],
                   w_uv[i], w_branch_a[i], w_branch_b[i], w_gate[i], b_gate[i], w_out[i], ln1_g[i], ln1_b[i],
                   peer_wq[i], peer_subkeys[i], peer_u[i], peer_v[i], w_ple[i], w_ple_gate[i], b_ple_gate[i],
                   ln2_g[i], ln2_b[i], bsz, seq, alpha)
    return h.reshape(bsz, seq, d)
```

```python
import functools

import jax
import jax.numpy as jnp
from jax import lax
from jax.experimental import pallas as pl
from jax.experimental.pallas import tpu as pltpu

F32 = jnp.float32
BF16 = jnp.bfloat16
I32 = jnp.int32

LANES = 128
SUBLANES = 8
CHUNK = 64
A_HEADS = 16
A_HEAD_DIM = 128
A_LATENT = 256
IDX_HEADS = 32
IDX_DIM = 64
TOPK_MAX = 256
B_HEADS = 16
B_HEAD_DIM = 128
PEER_TOPK = 16
LN_EPS = 1e-5
INT_MIN = -2 ** 31
MIB = 1024 * 1024

VMEM_MATMUL_MIB = 56
VMEM_STREAM_MIB = 48
VMEM_EXPERTS_MIB = 60
MM_TILE = 1024
MERGE_COLS = 256
LN_ROWS = 256
FOX_BLOCK = 512
PEER_TOKENS = 512
PEER_EXPERTS = 512
PEER_OUT_ROWS = 256
PEER_GATE_ROWS = 32


def _params(n_grid, vmem_mib):
    return pltpu.CompilerParams(dimension_semantics=("arbitrary",) * n_grid,
                                vmem_limit_bytes=vmem_mib * MIB)


def _dot(a, b):
    return jnp.dot(a, b, preferred_element_type=F32)


def _dot_nt(a, b):
    return lax.dot_general(a, b, (((1,), (1,)), ((), ())), preferred_element_type=F32)


def _mm_kernel(a_ref, b_ref, o_ref, *, groups):
    acc = _dot(a_ref[...], b_ref[...])
    if groups is None:
        o_ref[...] = acc.astype(o_ref.dtype)
    else:
        for g in range(groups):
            o_ref[g] = acc[:, g * LANES:(g + 1) * LANES].astype(o_ref.dtype)


def _matmul(a, b, out_dtype, bm, bn, head_major=False, name="mm"):
    m, k = a.shape
    n = b.shape[1]
    bm, bn = min(bm, m), min(bn, n)
    assert m % bm == 0 and n % bn == 0
    if head_major:
        out_shape = jax.ShapeDtypeStruct((n // LANES, m, LANES), out_dtype)
        out_spec = pl.BlockSpec((bn // LANES, bm, LANES), lambda j, i: (j, i, 0))
        groups = bn // LANES
    else:
        out_shape = jax.ShapeDtypeStruct((m, n), out_dtype)
        out_spec = pl.BlockSpec((bm, bn), lambda j, i: (i, j))
        groups = None
    return pl.pallas_call(
        functools.partial(_mm_kernel, groups=groups),
        grid=(n // bn, m // bm),
        in_specs=[pl.BlockSpec((bm, k), lambda j, i: (i, 0)),
                  pl.BlockSpec((k, bn), lambda j, i: (0, j))],
        out_specs=out_spec, out_shape=out_shape,
        compiler_params=_params(2, VMEM_MATMUL_MIB), name=name)(a, b)


def _mm_res_kernel(a_ref, b_ref, x_ref, o_ref, *, alpha):
    o_ref[...] = alpha * x_ref[...] + _dot(a_ref[...], b_ref[...])


def _matmul_residual(a, b, x, alpha, bm, bn):
    m, k = a.shape
    n = b.shape[1]
    bm, bn = min(bm, m), min(bn, n)
    return pl.pallas_call(
        functools.partial(_mm_res_kernel, alpha=alpha),
        grid=(n // bn, m // bm),
        in_specs=[pl.BlockSpec((bm, k), lambda j, i: (i, 0)),
                  pl.BlockSpec((k, bn), lambda j, i: (0, j)),
                  pl.BlockSpec((bm, bn), lambda j, i: (i, j))],
        out_specs=pl.BlockSpec((bm, bn), lambda j, i: (i, j)),
        out_shape=jax.ShapeDtypeStruct((m, n), F32),
        compiler_params=_params(2, VMEM_MATMUL_MIB), name="outproj")(a, b, x)


def _merge_kernel(x_ref, oa_ref, ob_ref, wga_ref, wgb_ref, bga_ref, bgb_ref, wba_ref, wbb_ref, o_ref):
    x = x_ref[...]
    ga = jax.nn.sigmoid(_dot(x, wga_ref[...]) + bga_ref[...])
    gb = jax.nn.sigmoid(_dot(x, wgb_ref[...]) + bgb_ref[...])
    ya = _dot(oa_ref[...], wba_ref[...])
    yb = _dot(ob_ref[...], wbb_ref[...])
    o_ref[...] = (ga * ya + gb * yb).astype(o_ref.dtype)


def _merge(xb, oa, ob, wg, bg, wba, wbb, bm, bn):
    m, d = xb.shape
    ka, kb = oa.shape[1], ob.shape[1]
    n = wba.shape[1]
    bm, bn = min(bm, m), min(bn, n)
    nj = n // bn
    row = lambda i, j: (i, 0)
    col = lambda i, j: (0, j)
    colb = lambda i, j: (0, nj + j)
    return pl.pallas_call(
        _merge_kernel,
        grid=(m // bm, nj),
        in_specs=[pl.BlockSpec((bm, d), row), pl.BlockSpec((bm, ka), row), pl.BlockSpec((bm, kb), row),
                  pl.BlockSpec((d, bn), col), pl.BlockSpec((d, bn), colb),
                  pl.BlockSpec((1, bn), col), pl.BlockSpec((1, bn), colb),
                  pl.BlockSpec((ka, bn), col), pl.BlockSpec((kb, bn), col)],
        out_specs=pl.BlockSpec((bm, bn), lambda i, j: (i, j)),
        out_shape=jax.ShapeDtypeStruct((m, n), BF16),
        compiler_params=_params(2, VMEM_MATMUL_MIB), name="merge")(xb, oa, ob, wg, wg, bg, bg, wba, wbb)


def _ple_kernel(h_ref, p_ref, wg_ref, bg_ref, wp_ref, o_ref):
    g = jax.nn.sigmoid(_dot(h_ref[...], wg_ref[...]) + bg_ref[...])
    o_ref[...] = (g * _dot(p_ref[...], wp_ref[...])).astype(o_ref.dtype)


def _ple(hb, pb, wg, bg, wp, bm, bn):
    m, d = hb.shape
    dp = pb.shape[1]
    n = wg.shape[1]
    bm, bn = min(bm, m), min(bn, n)
    return pl.pallas_call(
        _ple_kernel,
        grid=(n // bn, m // bm),
        in_specs=[pl.BlockSpec((bm, d), lambda j, i: (i, 0)), pl.BlockSpec((bm, dp), lambda j, i: (i, 0)),
                  pl.BlockSpec((d, bn), lambda j, i: (0, j)), pl.BlockSpec((1, bn), lambda j, i: (0, j)),
                  pl.BlockSpec((dp, bn), lambda j, i: (0, j))],
        out_specs=pl.BlockSpec((bm, bn), lambda j, i: (i, j)),
        out_shape=jax.ShapeDtypeStruct((m, n), BF16),
        compiler_params=_params(2, VMEM_MATMUL_MIB), name="ple")(hb, pb, wg, bg, wp)


def _transpose_cast_kernel(x_ref, o_ref):
    o_ref[...] = x_ref[...].T.astype(o_ref.dtype)


def _transpose_cast(x, dtype, br, bc):
    r, c = x.shape
    br, bc = min(br, r), min(bc, c)
    return pl.pallas_call(
        _transpose_cast_kernel,
        grid=(r // br, c // bc),
        in_specs=[pl.BlockSpec((br, bc), lambda i, j: (i, j))],
        out_specs=pl.BlockSpec((bc, br), lambda i, j: (j, i)),
        out_shape=jax.ShapeDtypeStruct((c, r), dtype),
        compiler_params=_params(2, VMEM_STREAM_MIB), name="transpose_cast")(x)


def _layer_norm_rows(v, g, b):
    mu = jnp.mean(v, axis=-1, keepdims=True)
    d = v - mu
    var = jnp.mean(d * d, axis=-1, keepdims=True)
    return d * lax.rsqrt(var + LN_EPS) * g + b


def _ln1_kernel(hp_ref, g_ref, b_ref, h_ref, hb_ref, ht_ref):
    y = _layer_norm_rows(hp_ref[...], g_ref[...], b_ref[...])
    h_ref[...] = y
    hb_ref[...] = y.astype(BF16)
    ht_ref[...] = y.T.astype(BF16)


def _ln1(hp, g, b, bm):
    m, d = hp.shape
    bm = min(bm, m)
    return pl.pallas_call(
        _ln1_kernel,
        grid=(m // bm,),
        in_specs=[pl.BlockSpec((bm, d), lambda i: (i, 0)),
                  pl.BlockSpec((1, d), lambda i: (0, 0)), pl.BlockSpec((1, d), lambda i: (0, 0))],
        out_specs=[pl.BlockSpec((bm, d), lambda i: (i, 0)), pl.BlockSpec((bm, d), lambda i: (i, 0)),
                   pl.BlockSpec((d, bm), lambda i: (0, i))],
        out_shape=[jax.ShapeDtypeStruct((m, d), F32), jax.ShapeDtypeStruct((m, d), BF16),
                   jax.ShapeDtypeStruct((d, m), BF16)],
        compiler_params=_params(1, VMEM_STREAM_MIB), name="ln1")(hp, g, b)


def _ln2_kernel(h_ref, yt_ref, ple_ref, g_ref, b_ref, o_ref, *, alpha):
    v = alpha * h_ref[...] + yt_ref[...].astype(F32).T + ple_ref[...].astype(F32)
    o_ref[...] = _layer_norm_rows(v, g_ref[...], b_ref[...])


def _ln2(h, yt, ple, g, b, alpha, bm):
    m, d = h.shape
    bm = min(bm, m)
    return pl.pallas_call(
        functools.partial(_ln2_kernel, alpha=alpha),
        grid=(m // bm,),
        in_specs=[pl.BlockSpec((bm, d), lambda i: (i, 0)), pl.BlockSpec((d, bm), lambda i: (0, i)),
                  pl.BlockSpec((bm, d), lambda i: (i, 0)),
                  pl.BlockSpec((1, d), lambda i: (0, 0)), pl.BlockSpec((1, d), lambda i: (0, 0))],
        out_specs=pl.BlockSpec((bm, d), lambda i: (i, 0)),
        out_shape=jax.ShapeDtypeStruct((m, d), F32),
        compiler_params=_params(1, VMEM_STREAM_MIB), name="ln2")(h, yt, ple, g, b)


def _prep_kernel(sm_ref, gl_ref, bf_ref, ckv_ref, kk_ref, w_ref, cum_ref, cumt_ref, *, idx_scale, blk):
    s = sm_ref.shape[0]
    c = sm_ref[:, 0:A_LATENT]
    ms = jnp.mean(c * c, axis=-1, keepdims=True)
    ckv_ref[...] = (c * lax.rsqrt(ms + LN_EPS) * gl_ref[...]).astype(BF16)
    o = A_LATENT
    kk_ref[...] = sm_ref[:, o:o + LANES].astype(BF16)
    w_ref[...] = sm_ref[:, o + LANES:o + 2 * LANES] * idx_scale
    f = sm_ref[:, o + 2 * LANES:o + 3 * LANES] + bf_ref[...]
    ls = jnp.minimum(f, 0.0) - jnp.log1p(jnp.exp(-jnp.abs(f)))
    r = lax.broadcasted_iota(I32, (blk, blk), 0)
    cc = lax.broadcasted_iota(I32, (blk, blk), 1)
    tri = jnp.where(r >= cc, 1.0, 0.0).astype(BF16)
    carry = jnp.zeros((1, LANES), F32)
    for kb in range(s // blk):
        xk = ls[kb * blk:(kb + 1) * blk]
        hi = xk.astype(BF16)
        r1 = xk - hi.astype(F32)
        mid = r1.astype(BF16)
        lo = (r1 - mid.astype(F32)).astype(BF16)
        ck = _dot(tri, hi) + _dot(tri, mid) + _dot(tri, lo) + carry
        cum_ref[kb * blk:(kb + 1) * blk, :] = ck
        carry = ck[blk - 1:blk, :]
    cumt_ref[...] = cum_ref[...].T


def _prep(small, g_latent, b_forget_pad, bsz, seq, idx_scale):
    n, w = small.shape
    blk = min(256, seq)
    return pl.pallas_call(
        functools.partial(_prep_kernel, idx_scale=idx_scale, blk=blk),
        grid=(bsz,),
        in_specs=[pl.BlockSpec((seq, w), lambda b: (b, 0)),
                  pl.BlockSpec((1, A_LATENT), lambda b: (0, 0)), pl.BlockSpec((1, LANES), lambda b: (0, 0))],
        out_specs=[pl.BlockSpec((None, seq, A_LATENT), lambda b: (b, 0, 0)),
                   pl.BlockSpec((None, seq, LANES), lambda b: (b, 0, 0)),
                   pl.BlockSpec((seq, LANES), lambda b: (b, 0)),
                   pl.BlockSpec((seq, LANES), lambda b: (b, 0)),
                   pl.BlockSpec((None, LANES, seq), lambda b: (b, 0, 0))],
        out_shape=[jax.ShapeDtypeStruct((bsz, seq, A_LATENT), BF16),
                   jax.ShapeDtypeStruct((bsz, seq, LANES), BF16),
                   jax.ShapeDtypeStruct((n, LANES), F32),
                   jax.ShapeDtypeStruct((n, LANES), F32),
                   jax.ShapeDtypeStruct((bsz, LANES, seq), F32)],
        compiler_params=_params(1, VMEM_STREAM_MIB), name="prep")(small, g_latent, b_forget_pad)


FOX_HEADS_PER_STEP = 4


def _fox_kernel(q_ref, k_ref, v_ref, cum_ref, cumr_ref, o_ref, *, tq, scale):
    hp = pl.program_id(1)
    i = pl.program_id(2)
    nh = FOX_HEADS_PER_STEP
    lane = lax.broadcasted_iota(I32, (tq, LANES), 1)
    qs = [(q_ref[e].astype(F32) * scale).astype(BF16) for e in range(nh)]

    def block(j, carry, diagonal):
        off = pl.multiple_of(j * tq, tq)
        cum_k = cum_ref[pl.ds(off, tq), :]
        out = []
        for e in range(nh):
            m, l, acc = carry[e]
            k = k_ref[e, pl.ds(off, tq), :]
            v = v_ref[e, pl.ds(off, tq), :]
            ck = jnp.sum(jnp.where(lane == hp * nh + e, cum_k, 0.0), axis=1, keepdims=True)
            s = _dot_nt(k, qs[e]) + (cumr_ref[e, i] - ck)
            if diagonal:
                krow = lax.broadcasted_iota(I32, (tq, tq), 0)
                qcol = lax.broadcasted_iota(I32, (tq, tq), 1)
                s = jnp.where(krow <= qcol, s, -jnp.inf)
            m_new = jnp.maximum(m, jnp.max(s, axis=0, keepdims=True))
            p = jnp.exp(s - m_new)
            a = jnp.exp(m - m_new)
            l = a * l + jnp.sum(p, axis=0, keepdims=True)
            vt = v.astype(F32).T.astype(BF16)
            acc = a * acc + _dot(vt, p.astype(BF16))
            out.append((m_new, l, acc))
        return tuple(out)

    init = (jnp.full((1, tq), -jnp.inf, F32), jnp.zeros((1, tq), F32), jnp.zeros((B_HEAD_DIM, tq), F32))
    carry = lax.fori_loop(0, i, functools.partial(block, diagonal=False), (init,) * nh)
    final = block(i, carry, True)
    for e in range(nh):
        _, l, acc = final[e]
        o_ref[:, e * B_HEAD_DIM:(e + 1) * B_HEAD_DIM] = (acc / l).T.astype(o_ref.dtype)


def _fox(big, cum, cumr, bsz, seq, tq, g_q, g_k, g_v):
    n = big.shape[1]
    tq = min(tq, seq)
    nq = seq // tq
    nh = FOX_HEADS_PER_STEP
    assert g_q % nh == 0 and g_k % nh == 0 and g_v % nh == 0 and B_HEADS % nh == 0
    return pl.pallas_call(
        functools.partial(_fox_kernel, tq=tq, scale=B_HEAD_DIM ** -0.5),
        grid=(bsz, B_HEADS // nh, nq),
        in_specs=[pl.BlockSpec((nh, tq, LANES), lambda b, h, i: (g_q // nh + h, b * nq + i, 0)),
                  pl.BlockSpec((nh, seq, LANES), lambda b, h, i: (g_k // nh + h, b, 0)),
                  pl.BlockSpec((nh, seq, LANES), lambda b, h, i: (g_v // nh + h, b, 0)),
                  pl.BlockSpec((seq, LANES), lambda b, h, i: (b, 0)),
                  pl.BlockSpec((None, nh, nq, 1, tq), lambda b, h, i: (b, h, 0, 0, 0))],
        out_specs=pl.BlockSpec((tq, nh * B_HEAD_DIM), lambda b, h, i: (b * nq + i, h)),
        out_shape=jax.ShapeDtypeStruct((n, B_HEADS * B_HEAD_DIM), BF16),
        compiler_params=_params(3, VMEM_STREAM_MIB), name="fox")(big, big, big, cum, cumr)


def _dsa_kernel(qi_ref, qa_ref, kk_ref, w_ref, ckv_ref, wuk_ref, wuv_ref, o_ref,
                qd_ref, wt_ref, key_ref, bias_ref, jstar_ref, qlat_ref, s_ref, mp_ref, lp_ref, p_ref, acc_ref,
                *, tq, seq, kc, k_sel):
    i = pl.program_id(1)
    t0 = i * tq
    nk = (t0 + tq + kc - 1) // kc
    npair = IDX_HEADS // 2
    rows_all = A_HEADS * tq
    slopes = [2.0 ** (-8.0 * (h + 1) / A_HEADS) for h in range(A_HEADS)]

    lane = lax.broadcasted_iota(I32, (tq, LANES), 1)
    for j in range(npair):
        a = qi_ref[j].astype(F32)
        qd_ref[j, 0:tq, :] = jnp.where(lane < IDX_DIM, a, 0.0).astype(BF16)
        qd_ref[j, tq:2 * tq, :] = jnp.where(lane >= IDX_DIM, a, 0.0).astype(BF16)
    wt_ref[...] = w_ref[...].T

    qpos = t0 + lax.broadcasted_iota(I32, (1, tq), 1)
    chunk_end = (qpos // CHUNK + 1) * CHUNK

    def idx_chunk(c, carry):
        kkc = kk_ref[c]
        acc = jnp.zeros((kc, tq), F32)
        for j in range(npair):
            out = _dot_nt(kkc, qd_ref[j])
            acc = acc + jnp.maximum(out[:, :tq], 0.0) * wt_ref[2 * j:2 * j + 1, :]
            acc = acc + jnp.maximum(out[:, tq:], 0.0) * wt_ref[2 * j + 1:2 * j + 2, :]
        kidx = c * kc + lax.broadcasted_iota(I32, (kc, tq), 0)
        bits = pltpu.bitcast(acc, I32)
        key = bits ^ ((bits >> 31) & 0x7FFFFFFF)
        key_ref[c] = jnp.where(kidx < chunk_end, key, INT_MIN)
        return carry

    lax.fori_loop(0, nk, idx_chunk, 0)

    sub = lax.broadcasted_iota(I32, (SUBLANES, tq), 0)

    def count(pred):
        nacc = 4

        def body(c, accs):
            kch = key_ref[c]
            accs = list(accs)
            for g in range(kc // SUBLANES):
                kg = kch[g * SUBLANES:(g + 1) * SUBLANES, :]
                accs[g % nacc] = accs[g % nacc] + jnp.where(pred(kg, c * kc + g * SUBLANES + sub), 1.0, 0.0)
            return tuple(accs)
        accs = lax.fori_loop(0, nk, body, (jnp.zeros((SUBLANES, tq), F32),) * nacc)
        acc = (accs[0] + accs[1]) + (accs[2] + accs[3])
        return jnp.broadcast_to(jnp.sum(acc, axis=0, keepdims=True), (SUBLANES, tq))

    def bis(it, tu):
        cand_u = tu | lax.shift_left(jnp.int32(1), 31 - it)
        cand_s = cand_u ^ INT_MIN
        cnt = count(lambda k, _: k >= cand_s)
        return jnp.where(cnt >= k_sel, cand_u, tu)

    tu = lax.fori_loop(0, 32, bis, jnp.zeros((SUBLANES, tq), I32))
    thr = tu ^ INT_MIN
    cnt_ge = count(lambda k, _: k >= thr)
    tie = jnp.where((cnt_ge > k_sel) & (thr != INT_MIN), 1, 0)
    jstar_ref[...] = jnp.full((SUBLANES, tq), seq, I32)

    @pl.when(jnp.max(tie) > 0)
    def _():
        need = k_sel - count(lambda k, _: k > thr)
        nbits = seq.bit_length()

        def bis2(it, jj):
            cand = jj | lax.shift_left(jnp.int32(1), nbits - 1 - it)
            f = count(lambda k, kidx: (k == thr) & (kidx < cand))
            return jnp.where((cand <= seq) & (f <= need), cand, jj)

        jstar_ref[...] = lax.fori_loop(0, nbits, bis2, jnp.zeros((SUBLANES, tq), I32))

    thr1 = thr[0:1, :]
    jst1 = jstar_ref[0:1, :]

    def bias_chunk(c, carry):
        kch = key_ref[c]
        kidx = c * kc + lax.broadcasted_iota(I32, (kc, tq), 0)
        sel = ((kch > thr1) | ((kch == thr1) & (kidx < jst1))) & (kidx < chunk_end)
        bias_ref[c] = jnp.where(sel, 0.0, -jnp.inf).T
        return carry

    lax.fori_loop(0, nk, bias_chunk, 0)

    for h in range(A_HEADS):
        qlat_ref[h * tq:(h + 1) * tq, :] = _dot(qa_ref[h], wuk_ref[h]).astype(BF16)
    mp_ref[...] = jnp.full((rows_all, LANES), -jnp.inf, F32)
    rowq = t0 + lax.broadcasted_iota(I32, (tq, kc), 0)
    colk = lax.broadcasted_iota(I32, (tq, kc), 1)

    def pass1(c, carry):
        s = _dot_nt(qlat_ref[...], ckv_ref[c])
        b = bias_ref[c]
        dist = jnp.abs(rowq - (colk + c * kc)).astype(F32)
        for h in range(A_HEADS):
            r = slice(h * tq, (h + 1) * tq)
            sh = s[r] - slopes[h] * dist + b
            s_ref[c, r, :] = sh
            mp = mp_ref[r, :]
            for u in range(kc // LANES):
                mp = jnp.maximum(mp, sh[:, u * LANES:(u + 1) * LANES])
            mp_ref[r, :] = mp
        return carry

    lax.fori_loop(0, nk, pass1, 0)
    m = jnp.max(mp_ref[...], axis=1, keepdims=True)
    mp_ref[...] = jnp.broadcast_to(m, (rows_all, LANES))
    lp_ref[...] = jnp.zeros((rows_all, LANES), F32)
    acc_ref[...] = jnp.zeros((rows_all, A_LATENT), F32)

    def pass2(c, carry):
        for h in range(A_HEADS):
            r = slice(h * tq, (h + 1) * tq)
            mrep = mp_ref[r, :]
            lp = lp_ref[r, :]
            for u in range(kc // LANES):
                p = jnp.exp(s_ref[c, r, u * LANES:(u + 1) * LANES] - mrep)
                lp = lp + p
                p_ref[r, u * LANES:(u + 1) * LANES] = p.astype(BF16)
            lp_ref[r, :] = lp
        acc_ref[...] += _dot(p_ref[...], ckv_ref[c])
        return carry

    lax.fori_loop(0, nk, pass2, 0)
    l = jnp.sum(lp_ref[...], axis=1, keepdims=True)
    olat = (acc_ref[...] / l).astype(BF16)
    for h in range(A_HEADS):
        o_ref[:, h * A_HEAD_DIM:(h + 1) * A_HEAD_DIM] = _dot(
            olat[h * tq:(h + 1) * tq], wuv_ref[h]).astype(o_ref.dtype)


def _dsa(big_i, big_a, kk, widx, ckv, wuk_t, wuv_h, bsz, seq):
    n = big_a.shape[1]
    tq = LANES
    assert seq % tq == 0
    kc = min(256, seq)
    nq, nkc = seq // tq, seq // kc
    k_sel = min(TOPK_MAX, seq // 4)
    npair = IDX_HEADS // 2
    rows_all = A_HEADS * tq
    kk = kk.reshape(bsz, nkc, kc, LANES)
    ckv = ckv.reshape(bsz, nkc, kc, A_LATENT)
    return pl.pallas_call(
        functools.partial(_dsa_kernel, tq=tq, seq=seq, kc=kc, k_sel=k_sel),
        grid=(bsz, nq),
        in_specs=[pl.BlockSpec((npair, tq, LANES), lambda b, i: (0, b * nq + i, 0)),
                  pl.BlockSpec((A_HEADS, tq, LANES), lambda b, i: (0, b * nq + i, 0)),
                  pl.BlockSpec((None, nkc, kc, LANES), lambda b, i: (b, 0, 0, 0)),
                  pl.BlockSpec((tq, LANES), lambda b, i: (b * nq + i, 0)),
                  pl.BlockSpec((None, nkc, kc, A_LATENT), lambda b, i: (b, 0, 0, 0)),
                  pl.BlockSpec((A_HEADS, A_HEAD_DIM, A_LATENT), lambda b, i: (0, 0, 0)),
                  pl.BlockSpec((A_HEADS, A_LATENT, A_HEAD_DIM), lambda b, i: (0, 0, 0))],
        out_specs=pl.BlockSpec((tq, A_HEADS * A_HEAD_DIM), lambda b, i: (b * nq + i, 0)),
        out_shape=jax.ShapeDtypeStruct((n, A_HEADS * A_HEAD_DIM), BF16),
        scratch_shapes=[pltpu.VMEM((npair, 2 * tq, LANES), BF16),
                        pltpu.VMEM((LANES, tq), F32),
                        pltpu.VMEM((nkc, kc, tq), I32),
                        pltpu.VMEM((nkc, tq, kc), F32),
                        pltpu.VMEM((SUBLANES, tq), I32),
                        pltpu.VMEM((rows_all, A_LATENT), BF16),
                        pltpu.VMEM((nkc, rows_all, kc), F32),
                        pltpu.VMEM((rows_all, LANES), F32),
                        pltpu.VMEM((rows_all, LANES), F32),
                        pltpu.VMEM((rows_all, kc), BF16),
                        pltpu.VMEM((rows_all, A_LATENT), F32)],
        compiler_params=_params(2, VMEM_MATMUL_MIB), name="dsa")(big_i, big_a, kk, widx, ckv, wuk_t, wuv_h)


def _topk_desc(x, k):
    out = []
    cur = x
    for _ in range(k):
        m = jnp.max(cur, axis=0, keepdims=True)
        out.append(m)
        cur = jnp.where(cur == m, -jnp.inf, cur)
    return out


def _route_kernel(qp_ref, sk_ref, t1_ref, e1_ref, s2_ref, e2_ref, v2_ref, cand_ref, *, heads, rows):
    keys = s2_ref.shape[1]
    k1 = PEER_TOPK + 1
    for h in range(heads):
        s1 = _dot_nt(sk_ref[h, 0], qp_ref[2 * h])
        s2 = _dot_nt(sk_ref[h, 1], qp_ref[2 * h + 1])
        v1 = _topk_desc(s1, k1)
        v2 = _topk_desc(s2, k1)
        for a in range(k1):
            v2_ref[a:a + 1, :] = v2[a]
        off = 0
        for a in range(k1):
            nb = k1 // (a + 1)
            cand_ref[off:off + nb, :] = v1[a] + v2_ref[0:nb, :]
            off += nb
        cand_ref[off:, :] = jnp.full((cand_ref.shape[0] - off, cand_ref.shape[1]), -jnp.inf, F32)
        cand = cand_ref[...]
        tops = _topk_desc(cand, k1)
        t_k = tops[PEER_TOPK - 1]
        t_mid = 0.5 * (t_k + tops[PEER_TOPK])
        mx = v1[0] + v2[0]
        z = jnp.sum(jnp.where(cand >= t_k, jnp.exp(cand - mx), 0.0), axis=0, keepdims=True)
        e1 = jnp.exp(s1 - v1[0]) / z
        t1 = t_mid - s1
        for blk in range(keys // rows):
            t1_ref[blk, h * rows:(h + 1) * rows, :] = t1[blk * rows:(blk + 1) * rows, :]
            e1_ref[blk, h * rows:(h + 1) * rows, :] = e1[blk * rows:(blk + 1) * rows, :]
        s2_ref[h] = s2
        e2_ref[h] = jnp.exp(s2 - v2[0])


def _route(qp, sk, tn, rows):
    g, n, _ = qp.shape
    heads = g // 2
    keys = sk.shape[2]
    tn = min(tn, n)
    nblk = keys // rows
    k1 = PEER_TOPK + 1
    ncand = sum(k1 // (a + 1) for a in range(k1))
    ncand = -(-ncand // SUBLANES) * SUBLANES
    hk = pl.BlockSpec((heads, keys, tn), lambda i: (0, 0, i))
    rk = pl.BlockSpec((nblk, heads * rows, tn), lambda i: (0, 0, i))
    sd = jax.ShapeDtypeStruct((heads, keys, n), F32)
    rd = jax.ShapeDtypeStruct((nblk, heads * rows, n), F32)
    return pl.pallas_call(
        functools.partial(_route_kernel, heads=heads, rows=rows),
        grid=(n // tn,),
        in_specs=[pl.BlockSpec((g, tn, LANES), lambda i: (0, i, 0)),
                  pl.BlockSpec(sk.shape, lambda i: (0, 0, 0, 0))],
        out_specs=[rk, rk, hk, hk],
        out_shape=[rd, rd, sd, sd],
        scratch_shapes=[pltpu.VMEM((-(-k1 // SUBLANES) * SUBLANES, tn), F32), pltpu.VMEM((ncand, tn), F32)],
        compiler_params=_params(1, VMEM_STREAM_MIB), name="route")(qp, sk)


def _expert_kernel(ht_ref, u_ref, vt_ref, t1_ref, e1_ref, s2_ref, e2_ref, y_ref,
                   act_ref, a_ref, yacc_ref, *, heads, rows, ne, dchunk, sb):
    t = pl.program_id(0)
    keys = s2_ref.shape[1]
    d, tn = yacc_ref.shape
    e_blk = act_ref.shape[1]
    act_new_ref, act_use_ref, a_new_ref, a_use_ref = act_ref.at[0], act_ref.at[1], a_ref.at[0], a_ref.at[1]

    @pl.when(t == 0)
    def _():
        act_new_ref[...] = jnp.zeros(act_new_ref.shape, F32)
        a_new_ref[...] = jnp.zeros(a_new_ref.shape, BF16)
        yacc_ref[...] = jnp.zeros(yacc_ref.shape, F32)

    a_use_ref[...] = a_new_ref[...]
    act_use_ref[...] = act_new_ref[...]

    @pl.when((t - 2) % ne == 0)
    def _():
        yacc_ref[...] = jnp.zeros(yacc_ref.shape, F32)

    def second_matmul(c, zero):
        rs = slice(c * dchunk, (c + 1) * dchunk)
        yacc_ref[rs, :] = yacc_ref[rs, :] + _dot(vt_ref[rs, :], a_use_ref[...])
        c0 = c * dchunk
        yacc_ref[c0:c0 + SUBLANES, 0:LANES] = yacc_ref[c0:c0 + SUBLANES, 0:LANES] + zero

    def gate(piece):
        lo = piece * sb
        r, q = lo // keys, lo % keys
        a = act_use_ref[lo:lo + sb, :]
        w = jnp.zeros_like(a)
        for h in range(heads):
            sel = s2_ref[h, q:q + sb, :] >= t1_ref[h * rows + r:h * rows + r + 1, :]
            g = e1_ref[h * rows + r:h * rows + r + 1, :] * e2_ref[h, q:q + sb, :]
            w = w + jnp.where(sel, g, 0.0)
        gelu = 0.5 * a * (1.0 + lax.erf(a * (2.0 ** -0.5)))
        out = w * gelu
        a_new_ref[lo:lo + sb, :] = out.astype(BF16)
        tile = out[0:SUBLANES, 0:LANES]
        for i in range(sb // SUBLANES):
            for j in range(tn // LANES):
                if i or j:
                    tile = tile + out[i * SUBLANES:(i + 1) * SUBLANES, j * LANES:(j + 1) * LANES]
        return tile

    def first_matmul(half):
        c0 = half * (tn // 2)
        act_new_ref[:, c0:c0 + tn // 2] = _dot(u_ref[...], ht_ref[:, c0:c0 + tn // 2])

    def gates_then_zero(pieces):
        tile = gate(pieces[0])
        for piece in pieces[1:]:
            tile = tile + gate(piece)
        bits = pltpu.bitcast(tile, jnp.uint32)
        return pltpu.bitcast((bits >> 16) >> 16, F32)

    nd, npieces = d // dchunk, e_blk // sb
    per = npieces // nd
    for c in range(nd):
        if c == 0:
            first_matmul(0)
        if c == nd // 2:
            first_matmul(1)
        second_matmul(c, gates_then_zero(list(range(c * per, (c + 1) * per))))

    @pl.when((t - 2) % ne == ne - 1)
    def _():
        y_ref[...] = yacc_ref[...].astype(y_ref.dtype)


def _experts(ht, u, vt, t1r, e1r, s2, e2, tn, e_blk):
    d, n = ht.shape
    heads, keys, _ = s2.shape
    rows = e_blk // keys
    tn = min(tn, n)
    ne = u.shape[0] // e_blk
    last = (n // tn) * ne - 1
    st1 = lambda t: jnp.minimum(t, last)
    st2 = lambda t: jnp.clip(t - 1, 0, last)
    st3 = lambda t: jnp.clip(t - 2, 0, last)
    return pl.pallas_call(
        functools.partial(_expert_kernel, heads=heads, rows=rows, ne=ne,
                          dchunk=min(PEER_OUT_ROWS, d), sb=PEER_GATE_ROWS),
        grid=(last + 3,),
        in_specs=[pl.BlockSpec((d, tn), lambda t: (0, st1(t) // ne)),
                  pl.BlockSpec((e_blk, d), lambda t: (st1(t) % ne, 0)),
                  pl.BlockSpec((d, e_blk), lambda t: (0, st3(t) % ne)),
                  pl.BlockSpec((None, heads * rows, tn), lambda t: (st2(t) % ne, 0, st2(t) // ne)),
                  pl.BlockSpec((None, heads * rows, tn), lambda t: (st2(t) % ne, 0, st2(t) // ne)),
                  pl.BlockSpec((heads, keys, tn), lambda t: (0, 0, st2(t) // ne)),
                  pl.BlockSpec((heads, keys, tn), lambda t: (0, 0, st2(t) // ne))],
        out_specs=pl.BlockSpec((d, tn), lambda t: (0, st3(t) // ne)),
        out_shape=jax.ShapeDtypeStruct((d, n), BF16),
        scratch_shapes=[pltpu.VMEM((2, e_blk, tn), F32), pltpu.VMEM((2, e_blk, tn), BF16),
                        pltpu.VMEM((d, tn), F32)],
        compiler_params=_params(1, VMEM_EXPERTS_MIB), name="experts")(ht, u, vt, t1r, e1r, s2, e2)


def _layer(h, hb, p_i, w_in, b_forget, g_latent, w_uk, w_uv, w_branch_a, w_branch_b, w_gate, b_gate, w_out,
           ln1_g, ln1_b, peer_wq, peer_subkeys, peer_u, peer_v, w_ple, w_ple_gate, b_ple_gate, ln2_g, ln2_b,
           bsz, seq, alpha):
    n, d = h.shape
    wa, wl, wi = A_HEADS * A_HEAD_DIM, A_LATENT, IDX_HEADS * IDX_DIM
    wb = B_HEADS * B_HEAD_DIM
    o = [0, wa, wa + wl, wa + wl + wi, wa + wl + wi + IDX_DIM, wa + wl + wi + IDX_DIM + IDX_HEADS]
    o += [o[-1] + wb, o[-1] + 2 * wb, o[-1] + 3 * wb, o[-1] + 3 * wb + B_HEADS]
    seg = [w_in[:, o[k]:o[k + 1]] for k in range(9)]
    (w_qa, w_ckv, w_qi, w_ki, w_wi, w_qb, w_kb, w_vb, w_f) = seg
    zeros = lambda c: jnp.zeros((d, c), w_in.dtype)
    w_small = jnp.concatenate(
        [w_ckv, w_ki, w_ki, w_wi, zeros(LANES - IDX_HEADS), w_f, zeros(LANES - B_HEADS)], axis=1).astype(BF16)
    w_fox = w_in[:, o[5]:o[8]].astype(BF16)

    big_a = _matmul(hb, w_qa.astype(BF16), BF16, MM_TILE, MM_TILE, head_major=True, name="proj_qa")
    big_i = _matmul(hb, w_qi.astype(BF16), BF16, MM_TILE, MM_TILE, head_major=True, name="proj_qidx")
    big_b = _matmul(hb, w_fox, BF16, MM_TILE, MM_TILE, head_major=True, name="proj_fox")
    small = _matmul(hb, w_small, F32, MM_TILE, w_small.shape[1], name="proj_small")

    bf_pad = jnp.zeros((1, LANES), F32).at[0, :B_HEADS].set(b_forget)
    idx_scale = (IDX_DIM ** -0.5) * (IDX_HEADS ** -0.5)
    ckv, kk, widx, cum, cumt = _prep(small, g_latent.reshape(1, -1), bf_pad, bsz, seq, idx_scale)

    tq = min(FOX_BLOCK, seq)
    cumr = cumt[:, :B_HEADS, :].reshape(bsz, B_HEADS, seq // tq, 1, tq)
    o_b = _fox(big_b, cum, cumr, bsz, seq, tq, 0, wb // LANES, 2 * wb // LANES)

    wuk_t = (jnp.transpose(w_uk, (1, 2, 0)) * (A_HEAD_DIM ** -0.5)).astype(BF16)
    wuv_h = jnp.transpose(w_uv, (1, 0, 2)).astype(BF16)
    o_a = _dsa(big_i, big_a, kk, widx, ckv, wuk_t, wuv_h, bsz, seq)

    merged = _merge(hb, o_a, o_b, w_gate.astype(BF16), b_gate.reshape(1, 2 * d),
                    w_branch_a.astype(BF16), w_branch_b.astype(BF16), MM_TILE, MERGE_COLS)
    hpre = _matmul_residual(merged, w_out.astype(BF16), h, alpha, MM_TILE, MM_TILE)
    h1, h1b, h1t = _ln1(hpre, ln1_g.reshape(1, d), ln1_b.reshape(1, d), LN_ROWS)

    heads, qd = peer_wq.shape[1], peer_wq.shape[2]
    keys = peer_subkeys.shape[2]
    qp = _matmul(h1b, peer_wq.reshape(d, heads * qd).astype(BF16), BF16, MM_TILE, MM_TILE, head_major=True,
                 name="peer_q")
    tn = min(PEER_TOKENS, n)
    e_blk = PEER_EXPERTS
    t1r, e1r, s2, e2 = _route(qp, peer_subkeys.astype(BF16), LANES, e_blk // keys)
    v_t = _transpose_cast(peer_v, BF16, MM_TILE, MM_TILE)
    yt = _experts(h1t, peer_u.astype(BF16), v_t, t1r, e1r, s2, e2, tn, e_blk)

    ple = _ple(h1b, p_i.astype(BF16), w_ple_gate.astype(BF16), b_ple_gate.reshape(1, d), w_ple.astype(BF16),
               MM_TILE, MM_TILE)
    return _ln2(h1, yt, ple, ln2_g.reshape(1, d), ln2_b.reshape(1, d), alpha, LN_ROWS)


def kernel(x, p, w_in, b_forget, g_latent, w_uk, w_uv, w_branch_a, w_branch_b, w_gate, b_gate, w_out, ln1_g,
           ln1_b, peer_wq, peer_subkeys, peer_u, peer_v, w_ple, w_ple_gate, b_ple_gate, ln2_g, ln2_b):
    bsz, seq, d = x.shape
    depth = w_in.shape[0]
    alpha = (2.0 * depth) ** 0.25
    h = x.reshape(bsz * seq, d)
    for i in range(depth):
        h = _layer(h, h.astype(BF16), p[i].reshape(bsz * seq, -1), w_in[i], b_forget[i], g_latent[i], w_uk[i],
                   w_uv[i], w_branch_a[i], w_branch_b[i], w_gate[i], b_gate[i], w_out[i], ln1_g[i], ln1_b[i],
                   peer_wq[i], peer_subkeys[i], peer_u[i], peer_v[i], w_ple[i], w_ple_gate[i], b_ple_gate[i],
                   ln2_g[i], ln2_b[i], bsz, seq, alpha)
    return h.reshape(bsz, seq, d)
```
